```python
import jax, jax.numpy as jnp
from jax import lax
import numpy as np

D_MODEL = 1024
BATCH = 4
SEQ = 8192
DEPTH = 1

CONV_CH = D_MODEL // 2
N_CONV_GROUPS = 8
HEAD_DIM = 64
N_HEADS = (D_MODEL // 2) // HEAD_DIM
ATTN_W = N_HEADS * HEAD_DIM
MIX_W = CONV_CH + ATTN_W
CONV_K = 3
D_FF = ((8 * D_MODEL) // 3 + 255) // 256 * 256
Q_BLOCK = 128
EPS = 1e-6
FORGET_BIAS_INIT = 3.0
IN_COLS = 3 * CONV_CH + 3 * ATTN_W + N_HEADS

kernel_name = "hymba_conv_fox_convffn_layer"


def rmsnorm(x, g):
    xf = x.astype(jnp.float32)
    y = xf * lax.rsqrt(jnp.mean(xf * xf, axis=-1, keepdims=True) + EPS)
    return (y * g.astype(jnp.float32)).astype(x.dtype)


def causal_dwconv(x, w):
    s = x.shape[1]
    xp = jnp.pad(x, ((0, 0), (CONV_K - 1, 0), (0, 0)))
    return sum(xp[:, j:j + s, :] * w[j] for j in range(CONV_K))


def fox_attention(q, k, v, log_f):
    b, s, h, dh = q.shape
    nblk = s // Q_BLOCK
    scale = 1.0 / np.sqrt(dh).astype(np.float32)
    F = jnp.transpose(jnp.cumsum(log_f, axis=1), (0, 2, 1))
    q_blocks = jnp.transpose(q.reshape(b, nblk, Q_BLOCK, h, dh), (1, 0, 2, 3, 4))
    f_blocks = jnp.transpose(F.reshape(b, h, nblk, Q_BLOCK), (2, 0, 1, 3))
    kpos = jnp.arange(s)

    def one_block(args):
        qb, fqb, i = args
        qpos = i * Q_BLOCK + jnp.arange(Q_BLOCK)
        sc = jnp.einsum('bqhd,bkhd->bhqk', qb, k,
                        preferred_element_type=jnp.float32) * scale
        logits = sc + fqb[..., None] - F[:, :, None, :]
        mask = kpos[None, :] <= qpos[:, None]
        logits = jnp.where(mask[None, None], logits, -jnp.inf)
        p = jax.nn.softmax(logits, axis=-1)
        return jnp.einsum('bhqk,bkhd->bqhd', p.astype(v.dtype), v)

    out = lax.map(one_block, (q_blocks, f_blocks, jnp.arange(nblk)))
    return jnp.transpose(out, (1, 0, 2, 3, 4)).reshape(b, s, h * dh)


def setup_inputs(seed: int = 0) -> dict:
    key = jax.random.key(seed)
    ks = jax.random.split(key, 16)
    nrm = lambda k, shape, sc: jax.random.normal(k, shape, jnp.float32) * sc
    gain = lambda k, shape: 1.0 + 0.02 * jax.random.normal(k, shape, jnp.float32)
    return {
        "x": jax.random.normal(ks[0], (BATCH, SEQ, D_MODEL), jnp.float32),
        "g_mix": gain(ks[1], (DEPTH, D_MODEL)),
        "w_in": nrm(ks[2], (DEPTH, D_MODEL, IN_COLS), D_MODEL ** -0.5),
        "b_f": FORGET_BIAS_INIT + 0.1 * jax.random.normal(ks[3], (DEPTH, N_HEADS), jnp.float32),
        "w_conv": nrm(ks[4], (DEPTH, CONV_K, CONV_CH), CONV_K ** -0.5),
        "g_conv_out": gain(ks[5], (DEPTH, CONV_CH)),
        "g_attn_out": gain(ks[6], (DEPTH, ATTN_W)),
        "w_o": nrm(ks[7], (DEPTH, MIX_W, D_MODEL), MIX_W ** -0.5),
        "g_ffn": gain(ks[8], (DEPTH, D_MODEL)),
        "w_up": nrm(ks[9], (DEPTH, D_MODEL, 2 * D_FF), D_MODEL ** -0.5),
        "w_ffn_conv": nrm(ks[10], (DEPTH, CONV_K, 2 * D_FF), CONV_K ** -0.5),
        "w_down": nrm(ks[11], (DEPTH, D_FF, D_MODEL), D_FF ** -0.5),
        "g_final": gain(ks[12], (D_MODEL,)),
    }


def reference(x, g_mix, w_in, b_f, w_conv, g_conv_out, g_attn_out, w_o,
              g_ffn, w_up, w_ffn_conv, w_down, g_final):
    b, s, _ = x.shape
    o1 = CONV_CH
    o2 = 2 * CONV_CH
    o3 = 3 * CONV_CH
    o4 = o3 + ATTN_W
    o5 = o4 + ATTN_W
    o6 = o5 + ATTN_W
    for l in range(DEPTH):
        h = rmsnorm(x, g_mix[l])
        z = h @ w_in[l]
        gb, gc, xc = z[..., :o1], z[..., o1:o2], z[..., o2:o3]
        q = z[..., o3:o4].reshape(b, s, N_HEADS, HEAD_DIM)
        k = z[..., o4:o5].reshape(b, s, N_HEADS, HEAD_DIM)
        v = z[..., o5:o6].reshape(b, s, N_HEADS, HEAD_DIM)
        f_logit = z[..., o6:]
        y_conv = gb * causal_dwconv(gc * xc, w_conv[l])
        log_f = jax.nn.log_sigmoid((f_logit + b_f[l]).astype(jnp.float32))
        y_attn = fox_attention(q, k, v, log_f)
        mix = jnp.concatenate([rmsnorm(y_conv, g_conv_out[l]),
                               rmsnorm(y_attn, g_attn_out[l])], axis=-1)
        x = x + mix @ w_o[l]
        h = rmsnorm(x, g_ffn[l])
        u = causal_dwconv(h @ w_up[l], w_ffn_conv[l])
        a, g = u[..., :D_FF], u[..., D_FF:]
        x = x + (jax.nn.silu(g) * a) @ w_down[l]
    return rmsnorm(x, g_final)
```

```python
import functools

import jax
import jax.numpy as jnp
from jax import lax
from jax.experimental import pallas as pl
from jax.experimental.pallas import tpu as pltpu

EPS = 1e-6
CONV_K = 3
HEAD_DIM = 64
LANES = 128
SUBLANES = 8
NEG_BIG = -1e30

ROW_TILE_IN = 512
ROW_TILE_POST = 512
ATTN_TILE = 512
FF_CHUNK = 256

BF16 = jnp.bfloat16
F32 = jnp.float32


def _dot(a, b):
    return jnp.dot(a, b, preferred_element_type=F32)


def _dot_nt(a, b):
    return lax.dot_general(a, b, (((1,), (1,)), ((), ())), preferred_element_type=F32)


def _split3(v):
    hi = v.astype(BF16).astype(F32)
    r1 = v - hi
    mid = r1.astype(BF16).astype(F32)
    lo = r1 - mid
    return hi, mid, lo


def _shift_rows(cur, prev8, k):
    ext = jnp.concatenate([prev8, cur], axis=0)
    return pltpu.roll(ext, k, axis=0)[SUBLANES:]


def _causal_conv3(cur, prev8, w):
    return (_shift_rows(cur, prev8, 2) * w[0:1] + _shift_rows(cur, prev8, 1) * w[1:2]
            + cur * w[2:3])


def _inproj_kernel(x_ref, gmix_ref, wg_ref, wqk_ref, wvt_ref, wf_ref, bf_ref, wconv_ref, gconv_ref,
                   mixc_ref, q_ref, k_ref, vt_ref, pcarry_ref, fcarry_ref, *, n_heads, conv_ch):
    tm = x_ref.shape[1]
    attn_w = n_heads * HEAD_DIM

    @pl.when(pl.program_id(1) == 0)
    def _():
        pcarry_ref[...] = jnp.zeros_like(pcarry_ref)
        fcarry_ref[...] = jnp.zeros_like(fcarry_ref)

    x = x_ref[0]
    ms = jnp.mean(x * x, axis=-1, keepdims=True)
    h = (x * lax.rsqrt(ms + EPS) * gmix_ref[...]).astype(BF16)

    zg = _dot(h, wg_ref[...])
    gb = zg[:, :conv_ch]
    p = zg[:, conv_ch:2 * conv_ch] * zg[:, 2 * conv_ch:]
    yc = gb * _causal_conv3(p, pcarry_ref[...], wconv_ref[...])
    pcarry_ref[...] = p[tm - SUBLANES:]
    msc = jnp.mean(yc * yc, axis=-1, keepdims=True)
    mixc_ref[0] = (yc * lax.rsqrt(msc + EPS) * gconv_ref[...]).astype(BF16)

    zf = _dot(h, wf_ref[...]) + bf_ref[...]
    logf = jnp.minimum(zf, 0.0) - jnp.log1p(jnp.exp(-jnp.abs(zf)))
    row = lax.broadcasted_iota(jnp.int32, (tm, tm), 0)
    col = lax.broadcasted_iota(jnp.int32, (tm, tm), 1)
    tri = jnp.where(row >= col, 1.0, 0.0).astype(BF16)
    l_hi, l_mid, l_lo = _split3(logf)
    fcum = (_dot(tri, l_hi.astype(BF16)) + _dot(tri, l_mid.astype(BF16))
            + _dot(tri, l_lo.astype(BF16)) + fcarry_ref[0:1])
    fcarry_ref[...] = jnp.broadcast_to(fcum[tm - 1:tm], fcarry_ref.shape)

    zqk = _dot(h, wqk_ref[...])
    lane = lax.broadcasted_iota(jnp.int32, (tm, LANES), 1)
    scale = 1.0 / (HEAD_DIM ** 0.5)
    for hd in range(n_heads):
        f_hi, f_mid, f_lo = _split3(jnp.broadcast_to(fcum[:, hd:hd + 1], (tm, LANES)))
        pair = hd // 2
        qb = zqk[:, pair * LANES:(pair + 1) * LANES] * scale
        kb = zqk[:, attn_w + pair * LANES:attn_w + (pair + 1) * LANES]
        if hd % 2:
            qb = pltpu.roll(qb, HEAD_DIM, axis=1)
            kb = pltpu.roll(kb, HEAD_DIM, axis=1)
        q_extra = jnp.where(lane == HEAD_DIM, f_hi,
                            jnp.where(lane == HEAD_DIM + 1, f_mid,
                                      jnp.where(lane == HEAD_DIM + 2, f_lo,
                                                jnp.where(lane < HEAD_DIM + 6, 1.0, 0.0))))
        k_extra = jnp.where(lane < HEAD_DIM + 3, 1.0,
                            jnp.where(lane == HEAD_DIM + 3, -f_hi,
                                      jnp.where(lane == HEAD_DIM + 4, -f_mid,
                                                jnp.where(lane == HEAD_DIM + 5, -f_lo, 0.0))))
        q_ref[0, hd] = jnp.where(lane < HEAD_DIM, qb, q_extra).astype(BF16)
        k_ref[0, hd] = jnp.where(lane < HEAD_DIM, kb, k_extra).astype(BF16)

    vt = _dot_nt(wvt_ref[...], h)
    ones_rows = jnp.where(lax.broadcasted_iota(jnp.int32, (LANES - HEAD_DIM, tm), 0) == 0, 1.0, 0.0)
    for hd in range(n_heads):
        vt_ref[0, hd] = jnp.concatenate(
            [vt[hd * HEAD_DIM:(hd + 1) * HEAD_DIM], ones_rows], axis=0).astype(BF16)


def _inproj(x, g_mix, w_in, b_f, w_conv, g_conv_out, *, n_heads, conv_ch):
    b, s, d = x.shape
    attn_w = n_heads * HEAD_DIM
    tm = ROW_TILE_IN
    assert s % tm == 0 and n_heads <= LANES and n_heads % 2 == 0
    o3 = 3 * conv_ch
    wg = w_in[:, :o3].astype(BF16)
    wqk = w_in[:, o3:o3 + 2 * attn_w].astype(BF16)
    wvt = w_in[:, o3 + 2 * attn_w:o3 + 3 * attn_w].T.astype(BF16)
    wf = jnp.pad(w_in[:, o3 + 3 * attn_w:], ((0, 0), (0, LANES - n_heads))).astype(BF16)
    bf = jnp.pad(b_f, (0, LANES - n_heads)).reshape(1, LANES)
    wconv = jnp.pad(w_conv, ((0, SUBLANES - CONV_K), (0, 0)))

    const = lambda shape: pl.BlockSpec(shape, lambda bi, i: (0,) * len(shape),
                                       pipeline_mode=pl.Buffered(1))
    head_rows = pl.BlockSpec((1, n_heads, tm, LANES), lambda bi, i: (bi, 0, i, 0))
    return pl.pallas_call(
        functools.partial(_inproj_kernel, n_heads=n_heads, conv_ch=conv_ch),
        grid=(b, s // tm),
        in_specs=[
            pl.BlockSpec((1, tm, d), lambda bi, i: (bi, i, 0)),
            const((1, d)), const(wg.shape), const(wqk.shape), const(wvt.shape), const(wf.shape),
            const((1, LANES)), const(wconv.shape), const((1, conv_ch)),
        ],
        out_specs=[
            pl.BlockSpec((1, tm, conv_ch), lambda bi, i: (bi, i, 0)),
            head_rows, head_rows,
            pl.BlockSpec((1, n_heads, LANES, tm), lambda bi, i: (bi, 0, 0, i)),
        ],
        out_shape=[
            jax.ShapeDtypeStruct((b, s, conv_ch), BF16),
            jax.ShapeDtypeStruct((b, n_heads, s, LANES), BF16),
            jax.ShapeDtypeStruct((b, n_heads, s, LANES), BF16),
            jax.ShapeDtypeStruct((b, n_heads, LANES, s), BF16),
        ],
        scratch_shapes=[pltpu.VMEM((SUBLANES, conv_ch), F32), pltpu.VMEM((SUBLANES, LANES), F32)],
        compiler_params=pltpu.CompilerParams(
            dimension_semantics=("arbitrary", "arbitrary"), vmem_limit_bytes=48 * 1024 * 1024),
        name="inproj",
    )(x, g_mix.reshape(1, d), wg, wqk, wvt, wf, bf, wconv, g_conv_out.reshape(1, conv_ch))


def _attn_kernel(q_ref, k_ref, vt_ref, o_ref):
    s = q_ref.shape[2]
    t = ATTN_TILE
    kpos = lax.broadcasted_iota(jnp.int32, (t, t), 0)
    qpos = lax.broadcasted_iota(jnp.int32, (t, t), 1)
    causal = kpos <= qpos

    def q_tile(qi, carry):
        q0 = pl.multiple_of(qi * t, t)
        q = q_ref[0, 0, pl.ds(q0, t), :]

        def kv_step(j, m, acc, masked):
            k0 = pl.multiple_of(j * t, t)
            st = _dot_nt(k_ref[0, 0, pl.ds(k0, t), :], q)
            if masked:
                st = jnp.where(causal, st, NEG_BIG)
            m_new = jnp.maximum(m, jnp.max(st, axis=0, keepdims=True))
            p = jnp.exp(st - m_new).astype(BF16)
            alpha = jnp.exp(m - m_new)
            return m_new, acc * alpha + _dot(vt_ref[0, 0, :, pl.ds(k0, t)], p)

        m0 = jnp.full((1, t), NEG_BIG, F32)
        acc0 = jnp.zeros((LANES, t), F32)
        m, acc = lax.fori_loop(0, qi, lambda j, c: kv_step(j, c[0], c[1], False), (m0, acc0))
        m, acc = kv_step(qi, m, acc, True)
        o = acc[:HEAD_DIM] / acc[HEAD_DIM:HEAD_DIM + 1]
        o_ref[0, :, pl.ds(q0, t)] = o.astype(o_ref.dtype)
        return carry

    lax.fori_loop(0, s // t, q_tile, 0)


def _attention(q_aug, k_aug, vt_aug):
    b, n_heads, s, _ = q_aug.shape
    assert s % ATTN_TILE == 0
    rows = pl.BlockSpec((1, 1, s, LANES), lambda bi, hi: (bi, hi, 0, 0))
    return pl.pallas_call(
        _attn_kernel,
        grid=(b, n_heads),
        in_specs=[rows, rows, pl.BlockSpec((1, 1, LANES, s), lambda bi, hi: (bi, hi, 0, 0))],
        out_specs=pl.BlockSpec((1, HEAD_DIM, s), lambda bi, hi: (bi, hi, 0)),
        out_shape=jax.ShapeDtypeStruct((b, n_heads * HEAD_DIM, s), BF16),
        compiler_params=pltpu.CompilerParams(
            dimension_semantics=("arbitrary", "arbitrary"), vmem_limit_bytes=48 * 1024 * 1024),
        name="fox_attention",
    )(q_aug, k_aug, vt_aug)


def _post_kernel(x_ref, mixc_ref, yt_ref, gattn_ref, wo_ref, gffn_ref, wup_ref, wfc_ref, wdown_ref,
                 gfin_ref, out_ref, ucarry_ref, *, d_ff, final_norm):
    tm = x_ref.shape[1]

    @pl.when(pl.program_id(1) == 0)
    def _():
        ucarry_ref[...] = jnp.zeros_like(ucarry_ref)

    yt = yt_ref[0].astype(F32)
    msa = jnp.mean(yt * yt, axis=0, keepdims=True)
    yn = (yt * lax.rsqrt(msa + EPS) * gattn_ref[...]).T.astype(BF16)
    mix = jnp.concatenate([mixc_ref[0], yn], axis=1)
    x1 = x_ref[0] + _dot(mix, wo_ref[...])

    ms = jnp.mean(x1 * x1, axis=-1, keepdims=True)
    h2 = (x1 * lax.rsqrt(ms + EPS) * gffn_ref[...]).astype(BF16)

    acc = x1
    for c in range(d_ff // FF_CHUNK):
        ca = slice(c * FF_CHUNK, (c + 1) * FF_CHUNK)
        cg = slice(d_ff + c * FF_CHUNK, d_ff + (c + 1) * FF_CHUNK)
        ua = _dot(h2, wup_ref[:, ca])
        ug = _dot(h2, wup_ref[:, cg])
        a = _causal_conv3(ua, ucarry_ref[:, ca], wfc_ref[:, ca])
        g = _causal_conv3(ug, ucarry_ref[:, cg], wfc_ref[:, cg])
        ucarry_ref[:, ca] = ua[tm - SUBLANES:]
        ucarry_ref[:, cg] = ug[tm - SUBLANES:]
        act = (g * jax.nn.sigmoid(g) * a).astype(BF16)
        acc = acc + _dot(act, wdown_ref[ca, :])

    if final_norm:
        msf = jnp.mean(acc * acc, axis=-1, keepdims=True)
        acc = acc * lax.rsqrt(msf + EPS) * gfin_ref[...]
    out_ref[0] = acc


def _post(x, mixc, yt, g_attn_out, w_o, g_ffn, w_up, w_ffn_conv, w_down, g_final, *, final_norm):
    b, s, d = x.shape
    conv_ch = mixc.shape[-1]
    attn_w = yt.shape[1]
    d_ff = w_down.shape[0]
    tm = ROW_TILE_POST
    assert s % tm == 0 and d_ff % FF_CHUNK == 0
    wfc = jnp.pad(w_ffn_conv, ((0, SUBLANES - CONV_K), (0, 0)))

    const = lambda shape: pl.BlockSpec(shape, lambda bi, i: (0,) * len(shape),
                                       pipeline_mode=pl.Buffered(1))
    return pl.pallas_call(
        functools.partial(_post_kernel, d_ff=d_ff, final_norm=final_norm),
        grid=(b, s // tm),
        in_specs=[
            pl.BlockSpec((1, tm, d), lambda bi, i: (bi, i, 0)),
            pl.BlockSpec((1, tm, conv_ch), lambda bi, i: (bi, i, 0)),
            pl.BlockSpec((1, attn_w, tm), lambda bi, i: (bi, 0, i)),
            const((attn_w, 1)), const(w_o.shape), const((1, d)), const(w_up.shape),
            const(wfc.shape), const(w_down.shape), const((1, d)),
        ],
        out_specs=pl.BlockSpec((1, tm, d), lambda bi, i: (bi, i, 0)),
        out_shape=jax.ShapeDtypeStruct((b, s, d), F32),
        scratch_shapes=[pltpu.VMEM((SUBLANES, 2 * d_ff), F32)],
        compiler_params=pltpu.CompilerParams(
            dimension_semantics=("arbitrary", "arbitrary"), vmem_limit_bytes=56 * 1024 * 1024),
        name="post_ffn",
    )(x, mixc, yt, g_attn_out.reshape(attn_w, 1), w_o.astype(BF16), g_ffn.reshape(1, d),
      w_up.astype(BF16), wfc, w_down.astype(BF16), g_final.reshape(1, d))


def kernel(x, g_mix, w_in, b_f, w_conv, g_conv_out, g_attn_out, w_o, g_ffn, w_up, w_ffn_conv, w_down,
           g_final):
    depth = w_in.shape[0]
    conv_ch = w_conv.shape[-1]
    n_heads = b_f.shape[-1]
    for l in range(depth):
        mixc, q_aug, k_aug, vt_aug = _inproj(x, g_mix[l], w_in[l], b_f[l], w_conv[l], g_conv_out[l],
                                             n_heads=n_heads, conv_ch=conv_ch)
        yt = _attention(q_aug, k_aug, vt_aug)
        x = _post(x, mixc, yt, g_attn_out[l], w_o[l], g_ffn[l], w_up[l], w_ffn_conv[l], w_down[l],
                  g_final, final_norm=(l == depth - 1))
    return x
```

```python
import functools

import jax
import jax.numpy as jnp
from jax import lax
from jax.experimental import pallas as pl
from jax.experimental.pallas import tpu as pltpu

EPS = 1e-6
CONV_K = 3
HEAD_DIM = 64
LANES = 128
SUBLANES = 8
NEG_BIG = -1e30
LOG2_E = 1.4426950408889634
V_ROWS = HEAD_DIM + 16

ROW_TILE_IN = 512
ROW_TILE_POST = 512
ATTN_Q_TILE = 512
ATTN_K_TILE = 256
ATTN_HEADS_PER_STEP = 2
FF_CHUNK = 256

BF16 = jnp.bfloat16
F32 = jnp.float32


def _dot(a, b):
    return jnp.dot(a, b, preferred_element_type=F32)


def _dot_nt(a, b):
    return lax.dot_general(a, b, (((1,), (1,)), ((), ())), preferred_element_type=F32)


def _split3(v):
    hi = v.astype(BF16).astype(F32)
    r1 = v - hi
    mid = r1.astype(BF16).astype(F32)
    lo = r1 - mid
    return hi, mid, lo


def _shift_rows(cur, prev8, k):
    ext = jnp.concatenate([prev8, cur], axis=0)
    return pltpu.roll(ext, k, axis=0)[SUBLANES:]


def _causal_conv3(cur, prev8, w):
    return (_shift_rows(cur, prev8, 2) * w[0:1] + _shift_rows(cur, prev8, 1) * w[1:2]
            + cur * w[2:3])


def _inproj_kernel(x_ref, gmix_ref, wg_ref, wqk_ref, wvt_ref, wf_ref, bf_ref, wconv_ref, gconv_ref,
                   mixc_ref, q_ref, k_ref, vt_ref, pcarry_ref, fcarry_ref, *, n_heads, conv_ch):
    tm = x_ref.shape[1]
    attn_w = n_heads * HEAD_DIM

    @pl.when(pl.program_id(1) == 0)
    def _():
        pcarry_ref[...] = jnp.zeros_like(pcarry_ref)
        fcarry_ref[...] = jnp.zeros_like(fcarry_ref)

    x = x_ref[0]
    ms = jnp.mean(x * x, axis=-1, keepdims=True)
    h = (x * lax.rsqrt(ms + EPS) * gmix_ref[...]).astype(BF16)

    zg = _dot(h, wg_ref[...])
    gb = zg[:, :conv_ch]
    p = zg[:, conv_ch:2 * conv_ch] * zg[:, 2 * conv_ch:]
    yc = gb * _causal_conv3(p, pcarry_ref[...], wconv_ref[...])
    pcarry_ref[...] = p[tm - SUBLANES:]
    msc = jnp.mean(yc * yc, axis=-1, keepdims=True)
    mixc_ref[0] = (yc * lax.rsqrt(msc + EPS) * gconv_ref[...]).astype(BF16)

    zf = _dot(h, wf_ref[...]) + bf_ref[...]
    logf = jnp.minimum(zf, 0.0) - jnp.log1p(jnp.exp(-jnp.abs(zf)))
    row = lax.broadcasted_iota(jnp.int32, (tm, tm), 0)
    col = lax.broadcasted_iota(jnp.int32, (tm, tm), 1)
    tri = jnp.where(row >= col, 1.0, 0.0).astype(BF16)
    l_hi, l_mid, l_lo = _split3(logf)
    fcum = (_dot(tri, l_hi.astype(BF16)) + _dot(tri, l_mid.astype(BF16))
            + _dot(tri, l_lo.astype(BF16)) + fcarry_ref[0:1])
    fcarry_ref[...] = jnp.broadcast_to(fcum[tm - 1:tm], fcarry_ref.shape)

    zqk = _dot(h, wqk_ref[...])
    lane = lax.broadcasted_iota(jnp.int32, (tm, LANES), 1)
    scale = LOG2_E / (HEAD_DIM ** 0.5)
    fcum2 = fcum * LOG2_E
    for hd in range(n_heads):
        f_hi, f_mid, f_lo = _split3(jnp.broadcast_to(fcum2[:, hd:hd + 1], (tm, LANES)))
        pair = hd // 2
        qb = zqk[:, pair * LANES:(pair + 1) * LANES] * scale
        kb = zqk[:, attn_w + pair * LANES:attn_w + (pair + 1) * LANES]
        if hd % 2:
            qb = pltpu.roll(qb, HEAD_DIM, axis=1)
            kb = pltpu.roll(kb, HEAD_DIM, axis=1)
        q_extra = jnp.where(lane == HEAD_DIM, f_hi,
                            jnp.where(lane == HEAD_DIM + 1, f_mid,
                                      jnp.where(lane == HEAD_DIM + 2, f_lo,
                                                jnp.where(lane < HEAD_DIM + 6, 1.0, 0.0))))
        k_extra = jnp.where(lane < HEAD_DIM + 3, 1.0,
                            jnp.where(lane == HEAD_DIM + 3, -f_hi,
                                      jnp.where(lane == HEAD_DIM + 4, -f_mid,
                                                jnp.where(lane == HEAD_DIM + 5, -f_lo, 0.0))))
        q_ref[0, hd] = jnp.where(lane < HEAD_DIM, qb, q_extra).astype(BF16)
        k_ref[0, hd] = jnp.where(lane < HEAD_DIM, kb, k_extra).astype(BF16)

    vt = _dot_nt(wvt_ref[...], h)
    ones_rows = jnp.where(lax.broadcasted_iota(jnp.int32, (V_ROWS - HEAD_DIM, tm), 0) == 0, 1.0, 0.0)
    for hd in range(n_heads):
        vt_ref[0, hd] = jnp.concatenate(
            [vt[hd * HEAD_DIM:(hd + 1) * HEAD_DIM], ones_rows], axis=0).astype(BF16)


def _inproj(x, g_mix, w_in, b_f, w_conv, g_conv_out, *, n_heads, conv_ch):
    b, s, d = x.shape
    attn_w = n_heads * HEAD_DIM
    tm = ROW_TILE_IN
    assert s % tm == 0 and n_heads <= LANES and n_heads % 2 == 0
    o3 = 3 * conv_ch
    wg = w_in[:, :o3].astype(BF16)
    wqk = w_in[:, o3:o3 + 2 * attn_w].astype(BF16)
    wvt = w_in[:, o3 + 2 * attn_w:o3 + 3 * attn_w].T.astype(BF16)
    wf = jnp.pad(w_in[:, o3 + 3 * attn_w:], ((0, 0), (0, LANES - n_heads))).astype(BF16)
    bf = jnp.pad(b_f, (0, LANES - n_heads)).reshape(1, LANES)
    wconv = jnp.pad(w_conv, ((0, SUBLANES - CONV_K), (0, 0)))

    const = lambda shape: pl.BlockSpec(shape, lambda bi, i: (0,) * len(shape),
                                       pipeline_mode=pl.Buffered(1))
    head_rows = pl.BlockSpec((1, n_heads, tm, LANES), lambda bi, i: (bi, 0, i, 0))
    return pl.pallas_call(
        functools.partial(_inproj_kernel, n_heads=n_heads, conv_ch=conv_ch),
        grid=(b, s // tm),
        in_specs=[
            pl.BlockSpec((1, tm, d), lambda bi, i: (bi, i, 0)),
            const((1, d)), const(wg.shape), const(wqk.shape), const(wvt.shape), const(wf.shape),
            const((1, LANES)), const(wconv.shape), const((1, conv_ch)),
        ],
        out_specs=[
            pl.BlockSpec((1, tm, conv_ch), lambda bi, i: (bi, i, 0)),
            head_rows, head_rows,
            pl.BlockSpec((1, n_heads, V_ROWS, tm), lambda bi, i: (bi, 0, 0, i)),
        ],
        out_shape=[
            jax.ShapeDtypeStruct((b, s, conv_ch), BF16),
            jax.ShapeDtypeStruct((b, n_heads, s, LANES), BF16),
            jax.ShapeDtypeStruct((b, n_heads, s, LANES), BF16),
            jax.ShapeDtypeStruct((b, n_heads, V_ROWS, s), BF16),
        ],
        scratch_shapes=[pltpu.VMEM((SUBLANES, conv_ch), F32), pltpu.VMEM((SUBLANES, LANES), F32)],
        compiler_params=pltpu.CompilerParams(
            dimension_semantics=("arbitrary", "arbitrary"), vmem_limit_bytes=48 * 1024 * 1024),
        name="inproj",
    )(x, g_mix.reshape(1, d), wg, wqk, wvt, wf, bf, wconv, g_conv_out.reshape(1, conv_ch))


def _attn_kernel(q_ref, k_ref, vt_ref, o_ref, s_ref, acc_ref):
    nh = q_ref.shape[1]
    s = q_ref.shape[2]
    tq, tk = ATTN_Q_TILE, ATTN_K_TILE
    kpos = lax.broadcasted_iota(jnp.int32, (tk, tq), 0)
    qpos = lax.broadcasted_iota(jnp.int32, (tk, tq), 1)

    def q_tile(qi, carry):
        q0 = pl.multiple_of(qi * tq, tq)
        qs = [q_ref[0, hh, pl.ds(q0, tq), :] for hh in range(nh)]

        def issue_scores(j, slot):
            k0 = pl.multiple_of(j * tk, tk)
            for hh in range(nh):
                s_ref[hh, slot] = _dot_nt(k_ref[0, hh, pl.ds(k0, tk), :], qs[hh])

        def consume(j, slot, ms, key_offset=None):
            k0 = pl.multiple_of(j * tk, tk)
            out = []
            for hh in range(nh):
                def tile():
                    st = s_ref[hh, slot]
                    if key_offset is None:
                        return st
                    return jnp.where(kpos + key_offset <= qpos, st, NEG_BIG)
                m_new = jnp.maximum(ms[hh], jnp.max(tile(), axis=0, keepdims=True))
                p = jnp.exp2(tile() - m_new).astype(BF16)
                alpha = jnp.exp2(ms[hh] - m_new)
                acc_ref[hh] = acc_ref[hh] * alpha + _dot(vt_ref[0, hh, :, pl.ds(k0, tk)], p)
                out.append(m_new)
            return tuple(out)

        def kv_pair(jj, ms):
            issue_scores(2 * jj + 1, 1)
            ms = consume(2 * jj, 0, ms)
            issue_scores(2 * jj + 2, 0)
            return consume(2 * jj + 1, 1, ms)

        acc_ref[...] = jnp.zeros_like(acc_ref)
        issue_scores(0, 0)
        ms = lax.fori_loop(0, qi, kv_pair, tuple(jnp.full((1, tq), NEG_BIG, F32) for _ in range(nh)))
        issue_scores(2 * qi + 1, 1)
        ms = consume(2 * qi, 0, ms, key_offset=0)
        consume(2 * qi + 1, 1, ms, key_offset=tk)
        for hh in range(nh):
            acc = acc_ref[hh]
            o = acc[:HEAD_DIM] / acc[HEAD_DIM:HEAD_DIM + 1]
            o_ref[0, hh * HEAD_DIM:(hh + 1) * HEAD_DIM, pl.ds(q0, tq)] = o.astype(o_ref.dtype)
        return carry

    lax.fori_loop(0, s // tq, q_tile, 0)


def _attention(q_aug, k_aug, vt_aug):
    b, n_heads, s, _ = q_aug.shape
    nh = ATTN_HEADS_PER_STEP
    tq, tk = ATTN_Q_TILE, ATTN_K_TILE
    assert s % tq == 0 and tq == 2 * tk and n_heads % nh == 0
    rows = pl.BlockSpec((1, nh, s, LANES), lambda bi, hi: (bi, hi, 0, 0))
    return pl.pallas_call(
        _attn_kernel,
        grid=(b, n_heads // nh),
        in_specs=[rows, rows, pl.BlockSpec((1, nh, V_ROWS, s), lambda bi, hi: (bi, hi, 0, 0))],
        out_specs=pl.BlockSpec((1, nh * HEAD_DIM, s), lambda bi, hi: (bi, hi, 0)),
        out_shape=jax.ShapeDtypeStruct((b, n_heads * HEAD_DIM, s), BF16),
        scratch_shapes=[pltpu.VMEM((nh, 2, tk, tq), F32), pltpu.VMEM((nh, V_ROWS, tq), F32)],
        compiler_params=pltpu.CompilerParams(
            dimension_semantics=("arbitrary", "arbitrary"), vmem_limit_bytes=48 * 1024 * 1024),
        name="fox_attention",
    )(q_aug, k_aug, vt_aug)


def _post_kernel(x_ref, mixc_ref, yt_ref, gattn_ref, wo_ref, gffn_ref, wup_ref, wfc_ref, wdown_ref,
                 gfin_ref, out_ref, ucarry_ref, *, d_ff, final_norm):
    tm = x_ref.shape[1]

    @pl.when(pl.program_id(1) == 0)
    def _():
        ucarry_ref[...] = jnp.zeros_like(ucarry_ref)

    yt = yt_ref[0].astype(F32)
    msa = jnp.mean(yt * yt, axis=0, keepdims=True)
    yn = (yt * lax.rsqrt(msa + EPS) * gattn_ref[...]).T.astype(BF16)
    mix = jnp.concatenate([mixc_ref[0], yn], axis=1)
    x1 = x_ref[0] + _dot(mix, wo_ref[...])

    ms = jnp.mean(x1 * x1, axis=-1, keepdims=True)
    h2 = (x1 * lax.rsqrt(ms + EPS) * gffn_ref[...]).astype(BF16)

    acc = x1
    for c in range(d_ff // FF_CHUNK):
        ca = slice(c * FF_CHUNK, (c + 1) * FF_CHUNK)
        cg = slice(d_ff + c * FF_CHUNK, d_ff + (c + 1) * FF_CHUNK)
        ua = _dot(h2, wup_ref[:, ca])
        ug = _dot(h2, wup_ref[:, cg])
        a = _causal_conv3(ua, ucarry_ref[:, ca], wfc_ref[:, ca])
        g = _causal_conv3(ug, ucarry_ref[:, cg], wfc_ref[:, cg])
        ucarry_ref[:, ca] = ua[tm - SUBLANES:]
        ucarry_ref[:, cg] = ug[tm - SUBLANES:]
        act = (g * jax.nn.sigmoid(g) * a).astype(BF16)
        acc = acc + _dot(act, wdown_ref[ca, :])

    if final_norm:
        msf = jnp.mean(acc * acc, axis=-1, keepdims=True)
        acc = acc * lax.rsqrt(msf + EPS) * gfin_ref[...]
    out_ref[0] = acc


def _post(x, mixc, yt, g_attn_out, w_o, g_ffn, w_up, w_ffn_conv, w_down, g_final, *, final_norm):
    b, s, d = x.shape
    conv_ch = mixc.shape[-1]
    attn_w = yt.shape[1]
    d_ff = w_down.shape[0]
    tm = ROW_TILE_POST
    assert s % tm == 0 and d_ff % FF_CHUNK == 0
    wfc = jnp.pad(w_ffn_conv, ((0, SUBLANES - CONV_K), (0, 0)))

    const = lambda shape: pl.BlockSpec(shape, lambda bi, i: (0,) * len(shape),
                                       pipeline_mode=pl.Buffered(1))
    return pl.pallas_call(
        functools.partial(_post_kernel, d_ff=d_ff, final_norm=final_norm),
        grid=(b, s // tm),
        in_specs=[
            pl.BlockSpec((1, tm, d), lambda bi, i: (bi, i, 0)),
            pl.BlockSpec((1, tm, conv_ch), lambda bi, i: (bi, i, 0)),
            pl.BlockSpec((1, attn_w, tm), lambda bi, i: (bi, 0, i)),
            const((attn_w, 1)), const(w_o.shape), const((1, d)), const(w_up.shape),
            const(wfc.shape), const(w_down.shape), const((1, d)),
        ],
        out_specs=pl.BlockSpec((1, tm, d), lambda bi, i: (bi, i, 0)),
        out_shape=jax.ShapeDtypeStruct((b, s, d), F32),
        scratch_shapes=[pltpu.VMEM((SUBLANES, 2 * d_ff), F32)],
        compiler_params=pltpu.CompilerParams(
            dimension_semantics=("arbitrary", "arbitrary"), vmem_limit_bytes=56 * 1024 * 1024),
        name="post_ffn",
    )(x, mixc, yt, g_attn_out.reshape(attn_w, 1), w_o.astype(BF16), g_ffn.reshape(1, d),
      w_up.astype(BF16), wfc, w_down.astype(BF16), g_final.reshape(1, d))


def kernel(x, g_mix, w_in, b_f, w_conv, g_conv_out, g_attn_out, w_o, g_ffn, w_up, w_ffn_conv, w_down,
           g_final):
    depth = w_in.shape[0]
    conv_ch = w_conv.shape[-1]
    n_heads = b_f.shape[-1]
    for l in range(depth):
        mixc, q_aug, k_aug, vt_aug = _inproj(x, g_mix[l], w_in[l], b_f[l], w_conv[l], g_conv_out[l],
                                             n_heads=n_heads, conv_ch=conv_ch)
        yt = _attention(q_aug, k_aug, vt_aug)
        x = _post(x, mixc, yt, g_attn_out[l], w_o[l], g_ffn[l], w_up[l], w_ffn_conv[l], w_down[l],
                  g_final, final_norm=(l == depth - 1))
    return x
```

```python
import functools

import jax
import jax.numpy as jnp
from jax import lax
from jax.experimental import pallas as pl
from jax.experimental.pallas import tpu as pltpu

EPS = 1e-6
CONV_K = 3
HEAD_DIM = 64
LANES = 128
SUBLANES = 8
NEG_BIG = -1e30
LOG2_E = 1.4426950408889634
V_ROWS = HEAD_DIM + 16

ROW_TILE_IN = 512
ROW_TILE_POST = 512
ATTN_Q_TILE = 512
ATTN_K_TILE = 256
ATTN_HEADS_PER_STEP = 2
FF_CHUNK = 256
FF_ROW_BLOCK = 64

BF16 = jnp.bfloat16
F32 = jnp.float32


def _dot(a, b):
    return jnp.dot(a, b, preferred_element_type=F32)


def _dot_nt(a, b):
    return lax.dot_general(a, b, (((1,), (1,)), ((), ())), preferred_element_type=F32)


def _split3(v):
    hi = v.astype(BF16).astype(F32)
    r1 = v - hi
    mid = r1.astype(BF16).astype(F32)
    lo = r1 - mid
    return hi, mid, lo


def _shift_rows(cur, prev8, k):
    ext = jnp.concatenate([prev8, cur], axis=0)
    return pltpu.roll(ext, k, axis=0)[SUBLANES:]


def _causal_conv3(cur, prev8, w):
    return (_shift_rows(cur, prev8, 2) * w[0:1] + _shift_rows(cur, prev8, 1) * w[1:2]
            + cur * w[2:3])


def _causal_conv3_ext(ext, w):
    return (pltpu.roll(ext, 2, axis=0)[SUBLANES:] * w[0:1] + pltpu.roll(ext, 1, axis=0)[SUBLANES:] * w[1:2]
            + ext[SUBLANES:] * w[2:3])


def _inproj_kernel(x_ref, gmix_ref, wg_ref, wqk_ref, wvt_ref, wf_ref, bf_ref, wconv_ref, gconv_ref,
                   mixc_ref, q_ref, k_ref, vt_ref, pcarry_ref, fcarry_ref, *, n_heads, conv_ch):
    tm = x_ref.shape[1]
    attn_w = n_heads * HEAD_DIM

    @pl.when(pl.program_id(1) == 0)
    def _():
        pcarry_ref[...] = jnp.zeros_like(pcarry_ref)
        fcarry_ref[...] = jnp.zeros_like(fcarry_ref)

    x = x_ref[0]
    ms = jnp.mean(x * x, axis=-1, keepdims=True)
    h = (x * lax.rsqrt(ms + EPS) * gmix_ref[...]).astype(BF16)

    zg = _dot(h, wg_ref[...])
    gb = zg[:, :conv_ch]
    p = zg[:, conv_ch:2 * conv_ch] * zg[:, 2 * conv_ch:]
    yc = gb * _causal_conv3(p, pcarry_ref[...], wconv_ref[...])
    pcarry_ref[...] = p[tm - SUBLANES:]
    msc = jnp.mean(yc * yc, axis=-1, keepdims=True)
    mixc_ref[0] = (yc * lax.rsqrt(msc + EPS) * gconv_ref[...]).astype(BF16)

    zf = _dot(h, wf_ref[...]) + bf_ref[...]
    logf = jnp.minimum(zf, 0.0) - jnp.log1p(jnp.exp(-jnp.abs(zf)))
    row = lax.broadcasted_iota(jnp.int32, (tm, tm), 0)
    col = lax.broadcasted_iota(jnp.int32, (tm, tm), 1)
    tri = jnp.where(row >= col, 1.0, 0.0).astype(BF16)
    l_hi, l_mid, l_lo = _split3(logf)
    fcum = (_dot(tri, l_hi.astype(BF16)) + _dot(tri, l_mid.astype(BF16))
            + _dot(tri, l_lo.astype(BF16)) + fcarry_ref[0:1])
    fcarry_ref[...] = jnp.broadcast_to(fcum[tm - 1:tm], fcarry_ref.shape)

    zqk = _dot(h, wqk_ref[...])
    lane = lax.broadcasted_iota(jnp.int32, (tm, LANES), 1)
    scale = LOG2_E / (HEAD_DIM ** 0.5)
    fcum2 = fcum * LOG2_E
    for hd in range(n_heads):
        f_hi, f_mid, f_lo = _split3(jnp.broadcast_to(fcum2[:, hd:hd + 1], (tm, LANES)))
        pair = hd // 2
        qb = zqk[:, pair * LANES:(pair + 1) * LANES] * scale
        kb = zqk[:, attn_w + pair * LANES:attn_w + (pair + 1) * LANES]
        if hd % 2:
            qb = pltpu.roll(qb, HEAD_DIM, axis=1)
            kb = pltpu.roll(kb, HEAD_DIM, axis=1)
        q_extra = jnp.where(lane == HEAD_DIM, f_hi,
                            jnp.where(lane == HEAD_DIM + 1, f_mid,
                                      jnp.where(lane == HEAD_DIM + 2, f_lo,
                                                jnp.where(lane < HEAD_DIM + 6, 1.0, 0.0))))
        k_extra = jnp.where(lane < HEAD_DIM + 3, 1.0,
                            jnp.where(lane == HEAD_DIM + 3, -f_hi,
                                      jnp.where(lane == HEAD_DIM + 4, -f_mid,
                                                jnp.where(lane == HEAD_DIM + 5, -f_lo, 0.0))))
        q_ref[0, hd] = jnp.where(lane < HEAD_DIM, qb, q_extra).astype(BF16)
        k_ref[0, hd] = jnp.where(lane < HEAD_DIM, kb, k_extra).astype(BF16)

    vt = _dot_nt(wvt_ref[...], h)
    ones_rows = jnp.where(lax.broadcasted_iota(jnp.int32, (V_ROWS - HEAD_DIM, tm), 0) == 0, 1.0, 0.0)
    for hd in range(n_heads):
        vt_ref[0, hd] = jnp.concatenate(
            [vt[hd * HEAD_DIM:(hd + 1) * HEAD_DIM], ones_rows], axis=0).astype(BF16)


def _inproj(x, g_mix, w_in, b_f, w_conv, g_conv_out, *, n_heads, conv_ch):
    b, s, d = x.shape
    attn_w = n_heads * HEAD_DIM
    tm = ROW_TILE_IN
    assert s % tm == 0 and n_heads <= LANES and n_heads % 2 == 0
    o3 = 3 * conv_ch
    wg = w_in[:, :o3].astype(BF16)
    wqk = w_in[:, o3:o3 + 2 * attn_w].astype(BF16)
    wvt = w_in[:, o3 + 2 * attn_w:o3 + 3 * attn_w].T.astype(BF16)
    wf = jnp.pad(w_in[:, o3 + 3 * attn_w:], ((0, 0), (0, LANES - n_heads))).astype(BF16)
    bf = jnp.pad(b_f, (0, LANES - n_heads)).reshape(1, LANES)
    wconv = jnp.pad(w_conv, ((0, SUBLANES - CONV_K), (0, 0)))

    const = lambda shape: pl.BlockSpec(shape, lambda bi, i: (0,) * len(shape),
                                       pipeline_mode=pl.Buffered(1))
    head_rows = pl.BlockSpec((1, n_heads, tm, LANES), lambda bi, i: (bi, 0, i, 0))
    return pl.pallas_call(
        functools.partial(_inproj_kernel, n_heads=n_heads, conv_ch=conv_ch),
        grid=(b, s // tm),
        in_specs=[
            pl.BlockSpec((1, tm, d), lambda bi, i: (bi, i, 0)),
            const((1, d)), const(wg.shape), const(wqk.shape), const(wvt.shape), const(wf.shape),
            const((1, LANES)), const(wconv.shape), const((1, conv_ch)),
        ],
        out_specs=[
            pl.BlockSpec((1, tm, conv_ch), lambda bi, i: (bi, i, 0)),
            head_rows, head_rows,
            pl.BlockSpec((1, n_heads, V_ROWS, tm), lambda bi, i: (bi, 0, 0, i)),
        ],
        out_shape=[
            jax.ShapeDtypeStruct((b, s, conv_ch), BF16),
            jax.ShapeDtypeStruct((b, n_heads, s, LANES), BF16),
            jax.ShapeDtypeStruct((b, n_heads, s, LANES), BF16),
            jax.ShapeDtypeStruct((b, n_heads, V_ROWS, s), BF16),
        ],
        scratch_shapes=[pltpu.VMEM((SUBLANES, conv_ch), F32), pltpu.VMEM((SUBLANES, LANES), F32)],
        compiler_params=pltpu.CompilerParams(
            dimension_semantics=("arbitrary", "arbitrary"), vmem_limit_bytes=48 * 1024 * 1024),
        name="inproj",
    )(x, g_mix.reshape(1, d), wg, wqk, wvt, wf, bf, wconv, g_conv_out.reshape(1, conv_ch))


def _attn_kernel(q_ref, k_ref, vt_ref, o_ref, s_ref, acc_ref):
    nh = q_ref.shape[1]
    s = q_ref.shape[2]
    tq, tk = ATTN_Q_TILE, ATTN_K_TILE
    kpos = lax.broadcasted_iota(jnp.int32, (tk, tq), 0)
    qpos = lax.broadcasted_iota(jnp.int32, (tk, tq), 1)

    def q_tile(qi, carry):
        q0 = pl.multiple_of(qi * tq, tq)
        qs = [q_ref[0, hh, pl.ds(q0, tq), :] for hh in range(nh)]

        def issue_scores(j, slot):
            k0 = pl.multiple_of(j * tk, tk)
            for hh in range(nh):
                s_ref[hh, slot] = _dot_nt(k_ref[0, hh, pl.ds(k0, tk), :], qs[hh])

        def consume(j, slot, ms, key_offset=None):
            k0 = pl.multiple_of(j * tk, tk)
            out = []
            for hh in range(nh):
                def tile():
                    st = s_ref[hh, slot]
                    if key_offset is None:
                        return st
                    return jnp.where(kpos + key_offset <= qpos, st, NEG_BIG)
                m_new = jnp.maximum(ms[hh], jnp.max(tile(), axis=0, keepdims=True))
                p = jnp.exp2(tile() - m_new).astype(BF16)
                alpha = jnp.exp2(ms[hh] - m_new)
                acc_ref[hh] = acc_ref[hh] * alpha + _dot(vt_ref[0, hh, :, pl.ds(k0, tk)], p)
                out.append(m_new)
            return tuple(out)

        def kv_pair(jj, ms):
            issue_scores(2 * jj + 1, 1)
            ms = consume(2 * jj, 0, ms)
            issue_scores(2 * jj + 2, 0)
            return consume(2 * jj + 1, 1, ms)

        acc_ref[...] = jnp.zeros_like(acc_ref)
        issue_scores(0, 0)
        ms = tuple(jnp.full((1, tq), NEG_BIG, F32) for _ in range(nh))
        odd = qi % 2
        ms = lax.cond(odd == 1, lambda m: kv_pair(0, m), lambda m: m, ms)
        ms = lax.fori_loop(0, qi // 2,
                           lambda i, m: kv_pair(odd + 2 * i + 1, kv_pair(odd + 2 * i, m)), ms)
        issue_scores(2 * qi + 1, 1)
        ms = consume(2 * qi, 0, ms, key_offset=0)
        consume(2 * qi + 1, 1, ms, key_offset=tk)
        for hh in range(nh):
            acc = acc_ref[hh]
            o = acc[:HEAD_DIM] / acc[HEAD_DIM:HEAD_DIM + 1]
            o_ref[0, hh * HEAD_DIM:(hh + 1) * HEAD_DIM, pl.ds(q0, tq)] = o.astype(o_ref.dtype)
        return carry

    lax.fori_loop(0, s // tq, q_tile, 0)


def _attention(q_aug, k_aug, vt_aug):
    b, n_heads, s, _ = q_aug.shape
    nh = ATTN_HEADS_PER_STEP
    tq, tk = ATTN_Q_TILE, ATTN_K_TILE
    assert s % tq == 0 and tq == 2 * tk and n_heads % nh == 0
    rows = pl.BlockSpec((1, nh, s, LANES), lambda bi, hi: (bi, hi, 0, 0))
    return pl.pallas_call(
        _attn_kernel,
        grid=(b, n_heads // nh),
        in_specs=[rows, rows, pl.BlockSpec((1, nh, V_ROWS, s), lambda bi, hi: (bi, hi, 0, 0))],
        out_specs=pl.BlockSpec((1, nh * HEAD_DIM, s), lambda bi, hi: (bi, hi, 0)),
        out_shape=jax.ShapeDtypeStruct((b, n_heads * HEAD_DIM, s), BF16),
        scratch_shapes=[pltpu.VMEM((nh, 2, tk, tq), F32), pltpu.VMEM((nh, V_ROWS, tq), F32)],
        compiler_params=pltpu.CompilerParams(
            dimension_semantics=("arbitrary", "arbitrary"), vmem_limit_bytes=48 * 1024 * 1024),
        name="fox_attention",
    )(q_aug, k_aug, vt_aug)


def _post_kernel(x_ref, mixc_ref, yt_ref, gattn_ref, wo_ref, gffn_ref, wup_ref, wfc_ref, wdown_ref,
                 gfin_ref, out_ref, ucarry_ref, h2_ref, up_ref, act_ref, *, d_ff, final_norm):
    tm = x_ref.shape[1]

    @pl.when(pl.program_id(1) == 0)
    def _():
        ucarry_ref[...] = jnp.zeros_like(ucarry_ref)

    yt = yt_ref[0].astype(F32)
    msa = jnp.mean(yt * yt, axis=0, keepdims=True)
    yn = (yt * lax.rsqrt(msa + EPS) * gattn_ref[...]).T.astype(BF16)
    mix = jnp.concatenate([mixc_ref[0], yn], axis=1)
    x1 = x_ref[0] + _dot(mix, wo_ref[...])

    ms = jnp.mean(x1 * x1, axis=-1, keepdims=True)
    h2 = (x1 * lax.rsqrt(ms + EPS) * gffn_ref[...]).astype(BF16)

    out_ref[0] = x1
    h2_ref[...] = h2
    n_chunks = d_ff // FF_CHUNK
    cols = lambda c: (slice(c * FF_CHUNK, (c + 1) * FF_CHUNK),
                      slice(d_ff + c * FF_CHUNK, d_ff + (c + 1) * FF_CHUNK))

    def issue_up(c):
        for half, cc in enumerate(cols(c)):
            up_ref[c % 2, half, :SUBLANES] = ucarry_ref[:, cc]
            up_ref[c % 2, half, SUBLANES:] = _dot(h2_ref[...], wup_ref[:, cc])

    def consume(c):
        ca, cg = cols(c)
        slot = c % 2
        for r0 in range(0, tm, FF_ROW_BLOCK):
            ext = slice(r0, r0 + FF_ROW_BLOCK + SUBLANES)
            a = _causal_conv3_ext(up_ref[slot, 0, ext], wfc_ref[:, ca])
            g = _causal_conv3_ext(up_ref[slot, 1, ext], wfc_ref[:, cg])
            act_ref[slot, r0:r0 + FF_ROW_BLOCK] = (g * jax.nn.sigmoid(g) * a).astype(BF16)
        ucarry_ref[:, ca] = up_ref[slot, 0, tm:]
        ucarry_ref[:, cg] = up_ref[slot, 1, tm:]
        out_ref[0] += _dot(act_ref[slot], wdown_ref[ca, :])

    issue_up(0)
    for c in range(n_chunks):
        if c + 1 < n_chunks:
            issue_up(c + 1)
        consume(c)

    if final_norm:
        acc = out_ref[0]
        msf = jnp.mean(acc * acc, axis=-1, keepdims=True)
        out_ref[0] = acc * lax.rsqrt(msf + EPS) * gfin_ref[...]


def _post(x, mixc, yt, g_attn_out, w_o, g_ffn, w_up, w_ffn_conv, w_down, g_final, *, final_norm):
    b, s, d = x.shape
    conv_ch = mixc.shape[-1]
    attn_w = yt.shape[1]
    d_ff = w_down.shape[0]
    tm = ROW_TILE_POST
    assert s % tm == 0 and d_ff % FF_CHUNK == 0
    wfc = jnp.pad(w_ffn_conv, ((0, SUBLANES - CONV_K), (0, 0)))

    const = lambda shape: pl.BlockSpec(shape, lambda bi, i: (0,) * len(shape),
                                       pipeline_mode=pl.Buffered(1))
    return pl.pallas_call(
        functools.partial(_post_kernel, d_ff=d_ff, final_norm=final_norm),
        grid=(b, s // tm),
        in_specs=[
            pl.BlockSpec((1, tm, d), lambda bi, i: (bi, i, 0)),
            pl.BlockSpec((1, tm, conv_ch), lambda bi, i: (bi, i, 0)),
            pl.BlockSpec((1, attn_w, tm), lambda bi, i: (bi, 0, i)),
            const((attn_w, 1)), const(w_o.shape), const((1, d)), const(w_up.shape),
            const(wfc.shape), const(w_down.shape), const((1, d)),
        ],
        out_specs=pl.BlockSpec((1, tm, d), lambda bi, i: (bi, i, 0)),
        out_shape=jax.ShapeDtypeStruct((b, s, d), F32),
        scratch_shapes=[pltpu.VMEM((SUBLANES, 2 * d_ff), F32), pltpu.VMEM((tm, d), BF16),
                        pltpu.VMEM((2, 2, tm + SUBLANES, FF_CHUNK), F32),
                        pltpu.VMEM((2, tm, FF_CHUNK), BF16)],
        compiler_params=pltpu.CompilerParams(
            dimension_semantics=("arbitrary", "arbitrary"), vmem_limit_bytes=56 * 1024 * 1024),
        name="post_ffn",
    )(x, mixc, yt, g_attn_out.reshape(attn_w, 1), w_o.astype(BF16), g_ffn.reshape(1, d),
      w_up.astype(BF16), wfc, w_down.astype(BF16), g_final.reshape(1, d))


def kernel(x, g_mix, w_in, b_f, w_conv, g_conv_out, g_attn_out, w_o, g_ffn, w_up, w_ffn_conv, w_down,
           g_final):
    depth = w_in.shape[0]
    conv_ch = w_conv.shape[-1]
    n_heads = b_f.shape[-1]
    for l in range(depth):
        mixc, q_aug, k_aug, vt_aug = _inproj(x, g_mix[l], w_in[l], b_f[l], w_conv[l], g_conv_out[l],
                                             n_heads=n_heads, conv_ch=conv_ch)
        yt = _attention(q_aug, k_aug, vt_aug)
        x = _post(x, mixc, yt, g_attn_out[l], w_o[l], g_ffn[l], w_up[l], w_ffn_conv[l], w_down[l],
                  g_final, final_norm=(l == depth - 1))
    return x
```

```python
import functools

import jax
import jax.numpy as jnp
from jax import lax
from jax.experimental import pallas as pl
from jax.experimental.pallas import tpu as pltpu

EPS = 1e-6
CONV_K = 3
HEAD_DIM = 64
LANES = 128
SUBLANES = 8
NEG_BIG = -1e30
LOG2_E = 1.4426950408889634
V_ROWS = HEAD_DIM + 16

ROW_TILE_IN = 512
ROW_TILE_POST = 512
ATTN_Q_TILE = 512
ATTN_K_TILE = 256
ATTN_HEADS_PER_STEP = 2
ATTN_SLOTS = 2
FF_CHUNK = 256
FF_UP_SLOTS = 3
FF_ROW_BLOCK = 64

BF16 = jnp.bfloat16
F32 = jnp.float32


def _dot(a, b):
    return jnp.dot(a, b, preferred_element_type=F32)


def _dot_nt(a, b):
    return lax.dot_general(a, b, (((1,), (1,)), ((), ())), preferred_element_type=F32)


def _split3(v):
    hi = v.astype(BF16).astype(F32)
    r1 = v - hi
    mid = r1.astype(BF16).astype(F32)
    lo = r1 - mid
    return hi, mid, lo


def _shift_rows(cur, prev8, k):
    ext = jnp.concatenate([prev8, cur], axis=0)
    return pltpu.roll(ext, k, axis=0)[SUBLANES:]


def _causal_conv3(cur, prev8, w):
    return (_shift_rows(cur, prev8, 2) * w[0:1] + _shift_rows(cur, prev8, 1) * w[1:2]
            + cur * w[2:3])


def _causal_conv3_ext(ext, w):
    return (pltpu.roll(ext, 2, axis=0)[SUBLANES:] * w[0:1] + pltpu.roll(ext, 1, axis=0)[SUBLANES:] * w[1:2]
            + ext[SUBLANES:] * w[2:3])


def _inproj_kernel(x_ref, gmix_ref, wg_ref, wqk_ref, wvt_ref, bft_ref, sel_ref, m0_ref, m1_ref,
                   wconv_ref, gconv_ref, mixc_ref, q_ref, k_ref, vt_ref, pcarry_ref, fcarry_ref,
                   *, n_heads, conv_ch):
    tm = x_ref.shape[1]
    attn_w = n_heads * HEAD_DIM
    gate_rows = fcarry_ref.shape[0]

    @pl.when(pl.program_id(1) == 0)
    def _():
        pcarry_ref[...] = jnp.zeros_like(pcarry_ref)
        fcarry_ref[...] = jnp.zeros_like(fcarry_ref)

    x = x_ref[0]
    ms = jnp.mean(x * x, axis=-1, keepdims=True)
    h = (x * lax.rsqrt(ms + EPS) * gmix_ref[...]).astype(BF16)

    vf = _dot_nt(wvt_ref[...], h)
    ones_rows = jnp.where(lax.broadcasted_iota(jnp.int32, (V_ROWS - HEAD_DIM, tm), 0) == 0, 1.0, 0.0)
    for hd in range(n_heads):
        vt_ref[0, hd] = jnp.concatenate(
            [vf[hd * HEAD_DIM:(hd + 1) * HEAD_DIM], ones_rows], axis=0).astype(BF16)

    zf = vf[attn_w:attn_w + gate_rows] + bft_ref[...]
    c = jnp.minimum(zf, 0.0) - jnp.log1p(jnp.exp(-jnp.abs(zf)))
    pos = lax.broadcasted_iota(jnp.int32, (gate_rows, tm), 1)
    d = 1
    while d < tm:
        c = c + jnp.where(pos >= d, pltpu.roll(c, d, axis=1), 0.0)
        d *= 2
    fcum = c + fcarry_ref[:, :1]
    fcarry_ref[...] = jnp.broadcast_to(fcum[:, tm - 1:tm], fcarry_ref.shape)

    f_hi, f_mid, f_lo = _split3(fcum * LOG2_E)
    pieces = jnp.concatenate(
        [f_hi, f_mid, f_lo, jnp.zeros((LANES - 3 * gate_rows, tm), F32)], axis=0).astype(BF16)

    zg = _dot(h, wg_ref[...])
    gb = zg[:, :conv_ch]
    p = zg[:, conv_ch:2 * conv_ch] * zg[:, 2 * conv_ch:]
    yc = gb * _causal_conv3(p, pcarry_ref[...], wconv_ref[...])
    pcarry_ref[...] = p[tm - SUBLANES:]
    msc = jnp.mean(yc * yc, axis=-1, keepdims=True)
    mixc_ref[0] = (yc * lax.rsqrt(msc + EPS) * gconv_ref[...]).astype(BF16)

    routed = _dot(sel_ref[...], pieces)
    extras = jnp.concatenate([jnp.zeros((HEAD_DIM, tm), F32), routed], axis=0).T
    lane = lax.broadcasted_iota(jnp.int32, (tm, LANES), 1)
    scale = LOG2_E / (HEAD_DIM ** 0.5)
    for pair in range(n_heads // 2):
        zp = _dot(h, wqk_ref[:, 2 * pair * LANES:2 * (pair + 1) * LANES])
        for hd in (2 * pair, 2 * pair + 1):
            qb = zp[:, :LANES] * scale
            kb = zp[:, LANES:]
            if hd % 2:
                qb = pltpu.roll(qb, HEAD_DIM, axis=1)
                kb = pltpu.roll(kb, HEAD_DIM, axis=1)
            m0 = m0_ref[hd:hd + 1, :]
            m1 = m1_ref[hd:hd + 1, :]
            q_ref[0, hd] = jnp.where(lane < HEAD_DIM, qb, extras * m0 + m1).astype(BF16)
            k_ref[0, hd] = jnp.where(lane < HEAD_DIM, kb, extras * m1 + m0).astype(BF16)


def _inproj(x, g_mix, w_in, b_f, w_conv, g_conv_out, *, n_heads, conv_ch):
    b, s, d = x.shape
    attn_w = n_heads * HEAD_DIM
    tm = ROW_TILE_IN
    gate_rows = -(-n_heads // SUBLANES) * SUBLANES
    extra = LANES - HEAD_DIM
    assert s % tm == 0 and n_heads % 2 == 0 and 6 * n_heads <= extra and 3 * gate_rows <= LANES
    o3 = 3 * conv_ch
    wg = w_in[:, :o3].astype(BF16)
    wqk = jnp.concatenate(
        [w_in[:, o3:o3 + attn_w].reshape(d, n_heads // 2, LANES),
         w_in[:, o3 + attn_w:o3 + 2 * attn_w].reshape(d, n_heads // 2, LANES)], axis=2,
    ).reshape(d, 2 * attn_w).astype(BF16)
    wvt = jnp.pad(w_in[:, o3 + 2 * attn_w:].T, ((0, 2 * SUBLANES - n_heads), (0, 0))).astype(BF16)
    bft = jnp.broadcast_to(jnp.pad(b_f, (0, gate_rows - n_heads))[:, None], (gate_rows, tm))
    wconv = jnp.pad(w_conv, ((0, SUBLANES - CONV_K), (0, 0)))
    hh = jnp.arange(n_heads)
    sel = jnp.zeros((extra, LANES), F32)
    m0 = jnp.zeros((n_heads, LANES), F32)
    m1 = jnp.zeros((n_heads, LANES), F32)
    for j in range(3):
        sel = sel.at[6 * hh + j, j * gate_rows + hh].set(1.0).at[6 * hh + 3 + j, j * gate_rows + hh].set(-1.0)
        m0 = m0.at[hh, HEAD_DIM + 6 * hh + j].set(1.0)
        m1 = m1.at[hh, HEAD_DIM + 6 * hh + 3 + j].set(1.0)

    const = lambda shape: pl.BlockSpec(shape, lambda bi, i: (0,) * len(shape),
                                       pipeline_mode=pl.Buffered(1))
    head_rows = pl.BlockSpec((1, n_heads, tm, LANES), lambda bi, i: (bi, 0, i, 0))
    return pl.pallas_call(
        functools.partial(_inproj_kernel, n_heads=n_heads, conv_ch=conv_ch),
        grid=(b, s // tm),
        in_specs=[
            pl.BlockSpec((1, tm, d), lambda bi, i: (bi, i, 0)),
            const((1, d)), const(wg.shape), const(wqk.shape), const(wvt.shape), const(bft.shape),
            const(sel.shape), const(m0.shape), const(m1.shape), const(wconv.shape), const((1, conv_ch)),
        ],
        out_specs=[
            pl.BlockSpec((1, tm, conv_ch), lambda bi, i: (bi, i, 0)),
            head_rows, head_rows,
            pl.BlockSpec((1, n_heads, V_ROWS, tm), lambda bi, i: (bi, 0, 0, i)),
        ],
        out_shape=[
            jax.ShapeDtypeStruct((b, s, conv_ch), BF16),
            jax.ShapeDtypeStruct((b, n_heads, s, LANES), BF16),
            jax.ShapeDtypeStruct((b, n_heads, s, LANES), BF16),
            jax.ShapeDtypeStruct((b, n_heads, V_ROWS, s), BF16),
        ],
        scratch_shapes=[pltpu.VMEM((SUBLANES, conv_ch), F32), pltpu.VMEM((gate_rows, LANES), F32)],
        compiler_params=pltpu.CompilerParams(
            dimension_semantics=("arbitrary", "arbitrary"), vmem_limit_bytes=48 * 1024 * 1024),
        name="inproj",
    )(x, g_mix.reshape(1, d), wg, wqk, wvt, bft, sel.astype(BF16), m0, m1, wconv,
      g_conv_out.reshape(1, conv_ch))


def _attn_kernel(q_ref, k_ref, vt_ref, o_ref, s_ref, acc_ref):
    nh = q_ref.shape[1]
    s = q_ref.shape[2]
    tq, tk = ATTN_Q_TILE, ATTN_K_TILE
    kpos = lax.broadcasted_iota(jnp.int32, (tk, tq), 0)
    qpos = lax.broadcasted_iota(jnp.int32, (tk, tq), 1)

    def q_tile(qi, carry):
        q0 = pl.multiple_of(qi * tq, tq)
        qs = [q_ref[0, hh, pl.ds(q0, tq), :] for hh in range(nh)]

        def issue_scores(j, slot):
            k0 = pl.multiple_of(j * tk, tk)
            maxima = []
            for hh in range(nh):
                st = _dot_nt(k_ref[0, hh, pl.ds(k0, tk), :], qs[hh])
                s_ref[hh, slot] = st
                maxima.append(jnp.max(st, axis=0, keepdims=True))
            return tuple(maxima)

        def consume(j, slot, ms, tile_max=None, key_offset=None):
            k0 = pl.multiple_of(j * tk, tk)
            out = []
            for hh in range(nh):
                def tile():
                    st = s_ref[hh, slot]
                    if key_offset is None:
                        return st
                    return jnp.where(kpos + key_offset <= qpos, st, NEG_BIG)
                tmax = tile_max[hh] if key_offset is None else jnp.max(tile(), axis=0, keepdims=True)
                m_new = jnp.maximum(ms[hh], tmax)
                p = jnp.exp2(tile() - m_new).astype(BF16)
                alpha = jnp.exp2(ms[hh] - m_new)
                acc_ref[hh] = acc_ref[hh] * alpha + _dot(vt_ref[0, hh, :, pl.ds(k0, tk)], p)
                out.append(m_new)
            return tuple(out)

        def kv_pair(jj, state):
            ms, max0 = state
            max1 = issue_scores(2 * jj + 1, 1)
            ms = consume(2 * jj, 0, ms, max0)
            max0 = issue_scores(2 * jj + 2, 0)
            return consume(2 * jj + 1, 1, ms, max1), max0

        acc_ref[...] = jnp.zeros_like(acc_ref)
        state = (tuple(jnp.full((1, tq), NEG_BIG, F32) for _ in range(nh)), issue_scores(0, 0))
        odd = qi % 2
        state = lax.cond(odd == 1, lambda st: kv_pair(0, st), lambda st: st, state)
        ms, _ = lax.fori_loop(0, qi // 2,
                              lambda i, st: kv_pair(odd + 2 * i + 1, kv_pair(odd + 2 * i, st)), state)
        issue_scores(2 * qi + 1, 1)
        ms = consume(2 * qi, 0, ms, key_offset=0)
        consume(2 * qi + 1, 1, ms, key_offset=tk)
        for hh in range(nh):
            acc = acc_ref[hh]
            o = acc[:HEAD_DIM] / acc[HEAD_DIM:HEAD_DIM + 1]
            o_ref[0, hh * HEAD_DIM:(hh + 1) * HEAD_DIM, pl.ds(q0, tq)] = o.astype(o_ref.dtype)
        return carry

    lax.fori_loop(0, s // tq, q_tile, 0)


def _attention(q_aug, k_aug, vt_aug):
    b, n_heads, s, _ = q_aug.shape
    nh = ATTN_HEADS_PER_STEP
    tq, tk = ATTN_Q_TILE, ATTN_K_TILE
    assert s % tq == 0 and tq == 2 * tk and n_heads % nh == 0
    rows = pl.BlockSpec((1, nh, s, LANES), lambda bi, hi: (bi, hi, 0, 0))
    return pl.pallas_call(
        _attn_kernel,
        grid=(b, n_heads // nh),
        in_specs=[rows, rows, pl.BlockSpec((1, nh, V_ROWS, s), lambda bi, hi: (bi, hi, 0, 0))],
        out_specs=pl.BlockSpec((1, nh * HEAD_DIM, s), lambda bi, hi: (bi, hi, 0)),
        out_shape=jax.ShapeDtypeStruct((b, n_heads * HEAD_DIM, s), BF16),
        scratch_shapes=[pltpu.VMEM((nh, ATTN_SLOTS, tk, tq), F32), pltpu.VMEM((nh, V_ROWS, tq), F32)],
        compiler_params=pltpu.CompilerParams(
            dimension_semantics=("arbitrary", "arbitrary"), vmem_limit_bytes=48 * 1024 * 1024),
        name="fox_attention",
    )(q_aug, k_aug, vt_aug)


def _post_kernel(x_ref, mixc_ref, yt_ref, gattn_ref, wo_ref, gffn_ref, wup_ref, wfc_ref, wdown_ref,
                 gfin_ref, out_ref, ucarry_ref, h2_ref, up_ref, act_ref, *, d_ff, final_norm):
    tm = x_ref.shape[1]

    @pl.when(pl.program_id(1) == 0)
    def _():
        ucarry_ref[...] = jnp.zeros_like(ucarry_ref)

    yt = yt_ref[0].astype(F32)
    msa = jnp.mean(yt * yt, axis=0, keepdims=True)
    yn = (yt * lax.rsqrt(msa + EPS) * gattn_ref[...]).T.astype(BF16)
    mix = jnp.concatenate([mixc_ref[0], yn], axis=1)
    x1 = x_ref[0] + _dot(mix, wo_ref[...])

    ms = jnp.mean(x1 * x1, axis=-1, keepdims=True)
    h2 = (x1 * lax.rsqrt(ms + EPS) * gffn_ref[...]).astype(BF16)

    out_ref[0] = x1
    h2_ref[...] = h2
    n_chunks = d_ff // FF_CHUNK
    cols = lambda c: (slice(c * FF_CHUNK, (c + 1) * FF_CHUNK),
                      slice(d_ff + c * FF_CHUNK, d_ff + (c + 1) * FF_CHUNK))

    def issue_up(c):
        for half, cc in enumerate(cols(c)):
            up_ref[c % FF_UP_SLOTS, half, :SUBLANES] = ucarry_ref[:, cc]
            up_ref[c % FF_UP_SLOTS, half, SUBLANES:] = _dot(h2_ref[...], wup_ref[:, cc])

    def gate_and_down(c):
        ca, cg = cols(c)
        slot = c % FF_UP_SLOTS
        for r0 in range(0, tm, FF_ROW_BLOCK):
            ext = slice(r0, r0 + FF_ROW_BLOCK + SUBLANES)
            a = _causal_conv3_ext(up_ref[slot, 0, ext], wfc_ref[:, ca])
            g = _causal_conv3_ext(up_ref[slot, 1, ext], wfc_ref[:, cg])
            act_ref[c % 2, r0:r0 + FF_ROW_BLOCK] = (g * jax.nn.sigmoid(g) * a).astype(BF16)
        ucarry_ref[:, ca] = up_ref[slot, 0, tm:]
        ucarry_ref[:, cg] = up_ref[slot, 1, tm:]
        out_ref[0] += _dot(act_ref[c % 2], wdown_ref[ca, :])

    for c in range(min(FF_UP_SLOTS - 1, n_chunks)):
        issue_up(c)
    for c in range(n_chunks):
        if c + FF_UP_SLOTS - 1 < n_chunks:
            issue_up(c + FF_UP_SLOTS - 1)
        gate_and_down(c)

    if final_norm:
        acc = out_ref[0]
        msf = jnp.mean(acc * acc, axis=-1, keepdims=True)
        out_ref[0] = acc * lax.rsqrt(msf + EPS) * gfin_ref[...]


def _post(x, mixc, yt, g_attn_out, w_o, g_ffn, w_up, w_ffn_conv, w_down, g_final, *, final_norm):
    b, s, d = x.shape
    conv_ch = mixc.shape[-1]
    attn_w = yt.shape[1]
    d_ff = w_down.shape[0]
    tm = ROW_TILE_POST
    assert s % tm == 0 and d_ff % FF_CHUNK == 0
    wfc = jnp.pad(w_ffn_conv, ((0, SUBLANES - CONV_K), (0, 0)))

    const = lambda shape: pl.BlockSpec(shape, lambda bi, i: (0,) * len(shape),
                                       pipeline_mode=pl.Buffered(1))
    return pl.pallas_call(
        functools.partial(_post_kernel, d_ff=d_ff, final_norm=final_norm),
        grid=(b, s // tm),
        in_specs=[
            pl.BlockSpec((1, tm, d), lambda bi, i: (bi, i, 0)),
            pl.BlockSpec((1, tm, conv_ch), lambda bi, i: (bi, i, 0)),
            pl.BlockSpec((1, attn_w, tm), lambda bi, i: (bi, 0, i)),
            const((attn_w, 1)), const(w_o.shape), const((1, d)), const(w_up.shape),
            const(wfc.shape), const(w_down.shape), const((1, d)),
        ],
        out_specs=pl.BlockSpec((1, tm, d), lambda bi, i: (bi, i, 0)),
        out_shape=jax.ShapeDtypeStruct((b, s, d), F32),
        scratch_shapes=[pltpu.VMEM((SUBLANES, 2 * d_ff), F32), pltpu.VMEM((tm, d), BF16),
                        pltpu.VMEM((FF_UP_SLOTS, 2, tm + SUBLANES, FF_CHUNK), F32),
                        pltpu.VMEM((2, tm, FF_CHUNK), BF16)],
        compiler_params=pltpu.CompilerParams(
            dimension_semantics=("arbitrary", "arbitrary"), vmem_limit_bytes=56 * 1024 * 1024),
        name="post_ffn",
    )(x, mixc, yt, g_attn_out.reshape(attn_w, 1), w_o.astype(BF16), g_ffn.reshape(1, d),
      w_up.astype(BF16), wfc, w_down.astype(BF16), g_final.reshape(1, d))


def kernel(x, g_mix, w_in, b_f, w_conv, g_conv_out, g_attn_out, w_o, g_ffn, w_up, w_ffn_conv, w_down,
           g_final):
    depth = w_in.shape[0]
    conv_ch = w_conv.shape[-1]
    n_heads = b_f.shape[-1]
    for l in range(depth):
        mixc, q_aug, k_aug, vt_aug = _inproj(x, g_mix[l], w_in[l], b_f[l], w_conv[l], g_conv_out[l],
                                             n_heads=n_heads, conv_ch=conv_ch)
        yt = _attention(q_aug, k_aug, vt_aug)
        x = _post(x, mixc, yt, g_attn_out[l], w_o[l], g_ffn[l], w_up[l], w_ffn_conv[l], w_down[l],
                  g_final, final_norm=(l == depth - 1))
    return x
```

```python
import functools

import jax
import jax.numpy as jnp
from jax import lax
from jax.experimental import pallas as pl
from jax.experimental.pallas import tpu as pltpu

EPS = 1e-6
CONV_K = 3
HEAD_DIM = 64
LANES = 128
SUBLANES = 8
NEG_BIG = -1e30
LOG2_E = 1.4426950408889634
V_ROWS = HEAD_DIM + 16

ROW_TILE_IN = 512
ROW_TILE_POST = 512
ATTN_Q_TILE = 512
ATTN_K_TILE = 256
ATTN_HEADS_PER_STEP = 2
ATTN_SLOTS = 2
FF_CHUNK = 512
FF_UP_SLOTS = 3
FF_ROW_BLOCK = 64

BF16 = jnp.bfloat16
F32 = jnp.float32


def _dot(a, b):
    return jnp.dot(a, b, preferred_element_type=F32)


def _dot_nt(a, b):
    return lax.dot_general(a, b, (((1,), (1,)), ((), ())), preferred_element_type=F32)


def _split3(v):
    hi = v.astype(BF16).astype(F32)
    r1 = v - hi
    mid = r1.astype(BF16).astype(F32)
    lo = r1 - mid
    return hi, mid, lo


def _shift_rows(cur, prev8, k):
    ext = jnp.concatenate([prev8, cur], axis=0)
    return pltpu.roll(ext, k, axis=0)[SUBLANES:]


def _causal_conv3(cur, prev8, w):
    return (_shift_rows(cur, prev8, 2) * w[0:1] + _shift_rows(cur, prev8, 1) * w[1:2]
            + cur * w[2:3])


def _causal_conv3_ext(ext, w):
    return (pltpu.roll(ext, 2, axis=0)[SUBLANES:] * w[0:1] + pltpu.roll(ext, 1, axis=0)[SUBLANES:] * w[1:2]
            + ext[SUBLANES:] * w[2:3])


def _inproj_kernel(x_ref, gmix_ref, wg_ref, wqk_ref, wvt_ref, bft_ref, sel_ref, m0_ref, m1_ref,
                   wconv_ref, gconv_ref, mixc_ref, q_ref, k_ref, vt_ref, pcarry_ref, fcarry_ref,
                   *, n_heads, conv_ch):
    tm = x_ref.shape[1]
    attn_w = n_heads * HEAD_DIM
    gate_rows = fcarry_ref.shape[0]

    @pl.when(pl.program_id(1) == 0)
    def _():
        pcarry_ref[...] = jnp.zeros_like(pcarry_ref)
        fcarry_ref[...] = jnp.zeros_like(fcarry_ref)

    x = x_ref[0]
    ms = jnp.mean(x * x, axis=-1, keepdims=True)
    h = (x * lax.rsqrt(ms + EPS) * gmix_ref[...]).astype(BF16)

    vf = _dot_nt(wvt_ref[...], h)
    ones_rows = jnp.where(lax.broadcasted_iota(jnp.int32, (V_ROWS - HEAD_DIM, tm), 0) == 0, 1.0, 0.0)
    for hd in range(n_heads):
        vt_ref[0, hd] = jnp.concatenate(
            [vf[hd * HEAD_DIM:(hd + 1) * HEAD_DIM], ones_rows], axis=0).astype(BF16)

    zf = vf[attn_w:attn_w + gate_rows] + bft_ref[...]
    c = jnp.minimum(zf, 0.0) - jnp.log1p(jnp.exp(-jnp.abs(zf)))
    pos = lax.broadcasted_iota(jnp.int32, (gate_rows, tm), 1)
    d = 1
    while d < tm:
        c = c + jnp.where(pos >= d, pltpu.roll(c, d, axis=1), 0.0)
        d *= 2
    fcum = c + fcarry_ref[:, :1]
    fcarry_ref[...] = jnp.broadcast_to(fcum[:, tm - 1:tm], fcarry_ref.shape)

    f_hi, f_mid, f_lo = _split3(fcum * LOG2_E)
    pieces = jnp.concatenate(
        [f_hi, f_mid, f_lo, jnp.zeros((LANES - 3 * gate_rows, tm), F32)], axis=0).astype(BF16)

    zg = _dot(h, wg_ref[...])
    gb = zg[:, :conv_ch]
    p = zg[:, conv_ch:2 * conv_ch] * zg[:, 2 * conv_ch:]
    yc = gb * _causal_conv3(p, pcarry_ref[...], wconv_ref[...])
    pcarry_ref[...] = p[tm - SUBLANES:]
    msc = jnp.mean(yc * yc, axis=-1, keepdims=True)
    mixc_ref[0] = (yc * lax.rsqrt(msc + EPS) * gconv_ref[...]).astype(BF16)

    routed = _dot(sel_ref[...], pieces)
    extras = jnp.concatenate([jnp.zeros((HEAD_DIM, tm), F32), routed], axis=0).T
    lane = lax.broadcasted_iota(jnp.int32, (tm, LANES), 1)
    scale = LOG2_E / (HEAD_DIM ** 0.5)
    for pair in range(n_heads // 2):
        zp = _dot(h, wqk_ref[:, 2 * pair * LANES:2 * (pair + 1) * LANES])
        for hd in (2 * pair, 2 * pair + 1):
            qb = zp[:, :LANES] * scale
            kb = zp[:, LANES:]
            if hd % 2:
                qb = pltpu.roll(qb, HEAD_DIM, axis=1)
                kb = pltpu.roll(kb, HEAD_DIM, axis=1)
            m0 = m0_ref[hd:hd + 1, :]
            m1 = m1_ref[hd:hd + 1, :]
            q_ref[0, hd] = jnp.where(lane < HEAD_DIM, qb, extras * m0 + m1).astype(BF16)
            k_ref[0, hd] = jnp.where(lane < HEAD_DIM, kb, extras * m1 + m0).astype(BF16)


def _inproj(x, g_mix, w_in, b_f, w_conv, g_conv_out, *, n_heads, conv_ch):
    b, s, d = x.shape
    attn_w = n_heads * HEAD_DIM
    tm = ROW_TILE_IN
    gate_rows = -(-n_heads // SUBLANES) * SUBLANES
    extra = LANES - HEAD_DIM
    assert s % tm == 0 and n_heads % 2 == 0 and 6 * n_heads <= extra and 3 * gate_rows <= LANES
    o3 = 3 * conv_ch
    wg = w_in[:, :o3].astype(BF16)
    wqk = jnp.concatenate(
        [w_in[:, o3:o3 + attn_w].reshape(d, n_heads // 2, LANES),
         w_in[:, o3 + attn_w:o3 + 2 * attn_w].reshape(d, n_heads // 2, LANES)], axis=2,
    ).reshape(d, 2 * attn_w).astype(BF16)
    wvt = jnp.pad(w_in[:, o3 + 2 * attn_w:].T, ((0, 2 * SUBLANES - n_heads), (0, 0))).astype(BF16)
    bft = jnp.broadcast_to(jnp.pad(b_f, (0, gate_rows - n_heads))[:, None], (gate_rows, tm))
    wconv = jnp.pad(w_conv, ((0, SUBLANES - CONV_K), (0, 0)))
    hh = jnp.arange(n_heads)
    sel = jnp.zeros((extra, LANES), F32)
    m0 = jnp.zeros((n_heads, LANES), F32)
    m1 = jnp.zeros((n_heads, LANES), F32)
    for j in range(3):
        sel = sel.at[6 * hh + j, j * gate_rows + hh].set(1.0).at[6 * hh + 3 + j, j * gate_rows + hh].set(-1.0)
        m0 = m0.at[hh, HEAD_DIM + 6 * hh + j].set(1.0)
        m1 = m1.at[hh, HEAD_DIM + 6 * hh + 3 + j].set(1.0)

    const = lambda shape: pl.BlockSpec(shape, lambda bi, i: (0,) * len(shape),
                                       pipeline_mode=pl.Buffered(1))
    head_rows = pl.BlockSpec((1, n_heads, tm, LANES), lambda bi, i: (bi, 0, i, 0))
    return pl.pallas_call(
        functools.partial(_inproj_kernel, n_heads=n_heads, conv_ch=conv_ch),
        grid=(b, s // tm),
        in_specs=[
            pl.BlockSpec((1, tm, d), lambda bi, i: (bi, i, 0)),
            const((1, d)), const(wg.shape), const(wqk.shape), const(wvt.shape), const(bft.shape),
            const(sel.shape), const(m0.shape), const(m1.shape), const(wconv.shape), const((1, conv_ch)),
        ],
        out_specs=[
            pl.BlockSpec((1, tm, conv_ch), lambda bi, i: (bi, i, 0)),
            head_rows, head_rows,
            pl.BlockSpec((1, n_heads, V_ROWS, tm), lambda bi, i: (bi, 0, 0, i)),
        ],
        out_shape=[
            jax.ShapeDtypeStruct((b, s, conv_ch), BF16),
            jax.ShapeDtypeStruct((b, n_heads, s, LANES), BF16),
            jax.ShapeDtypeStruct((b, n_heads, s, LANES), BF16),
            jax.ShapeDtypeStruct((b, n_heads, V_ROWS, s), BF16),
        ],
        scratch_shapes=[pltpu.VMEM((SUBLANES, conv_ch), F32), pltpu.VMEM((gate_rows, LANES), F32)],
        compiler_params=pltpu.CompilerParams(
            dimension_semantics=("arbitrary", "arbitrary"), vmem_limit_bytes=48 * 1024 * 1024),
        name="inproj",
    )(x, g_mix.reshape(1, d), wg, wqk, wvt, bft, sel.astype(BF16), m0, m1, wconv,
      g_conv_out.reshape(1, conv_ch))


def _attn_kernel(q_ref, k_ref, vt_ref, o_ref, s_ref, acc_ref):
    nh = q_ref.shape[1]
    s = q_ref.shape[2]
    tq, tk = ATTN_Q_TILE, ATTN_K_TILE
    kpos = lax.broadcasted_iota(jnp.int32, (tk, tq), 0)
    qpos = lax.broadcasted_iota(jnp.int32, (tk, tq), 1)
    causal_sq = (lax.broadcasted_iota(jnp.int32, (tk, tq - tk), 0)
                 <= lax.broadcasted_iota(jnp.int32, (tk, tq - tk), 1))

    def q_tile(qi, carry):
        q0 = pl.multiple_of(qi * tq, tq)
        qs = [q_ref[0, hh, pl.ds(q0, tq), :] for hh in range(nh)]

        def issue_scores(j, slot):
            k0 = pl.multiple_of(j * tk, tk)
            maxima = []
            for hh in range(nh):
                st = _dot_nt(k_ref[0, hh, pl.ds(k0, tk), :], qs[hh])
                s_ref[hh, slot] = st
                maxima.append(jnp.max(st, axis=0, keepdims=True))
            return tuple(maxima)

        def consume(j, slot, ms, tile_max=None, key_offset=None):
            k0 = pl.multiple_of(j * tk, tk)
            out = []
            for hh in range(nh):
                def tile():
                    st = s_ref[hh, slot]
                    if key_offset is None:
                        return st
                    return jnp.where(kpos + key_offset <= qpos, st, NEG_BIG)
                tmax = tile_max[hh] if key_offset is None else jnp.max(tile(), axis=0, keepdims=True)
                m_new = jnp.maximum(ms[hh], tmax)
                p = jnp.exp2(tile() - m_new).astype(BF16)
                alpha = jnp.exp2(ms[hh] - m_new)
                acc_ref[hh] = acc_ref[hh] * alpha + _dot(vt_ref[0, hh, :, pl.ds(k0, tk)], p)
                out.append(m_new)
            return tuple(out)

        def kv_pair(jj, state):
            ms, max0 = state
            max1 = issue_scores(2 * jj + 1, 1)
            ms = consume(2 * jj, 0, ms, max0)
            max0 = issue_scores(2 * jj + 2, 0)
            return consume(2 * jj + 1, 1, ms, max1), max0

        acc_ref[...] = jnp.zeros_like(acc_ref)
        state = (tuple(jnp.full((1, tq), NEG_BIG, F32) for _ in range(nh)), issue_scores(0, 0))
        def pairs(j0, n, st):
            for u in range(n):
                st = kv_pair(j0 + u, st)
            return st

        n1 = qi % 2
        n2 = 2 * ((qi // 2) % 2)
        state = lax.cond(n1 == 1, lambda st: pairs(0, 1, st), lambda st: st, state)
        state = lax.cond(n2 == 2, lambda st: pairs(n1, 2, st), lambda st: st, state)
        ms, _ = lax.fori_loop(0, qi // 4, lambda i, st: pairs(n1 + n2 + 4 * i, 4, st), state)
        right = slice(tk, tq)
        k_last = pl.multiple_of((2 * qi + 1) * tk, tk)
        for hh in range(nh):
            s_ref[hh, 1, :, right] = _dot_nt(k_ref[0, hh, pl.ds(k_last, tk), :], qs[hh][tk:])
        ms = consume(2 * qi, 0, ms, key_offset=0)
        for hh in range(nh):
            st = jnp.where(causal_sq, s_ref[hh, 1, :, right], NEG_BIG)
            m_old = ms[hh][:, right]
            m_new = jnp.maximum(m_old, jnp.max(st, axis=0, keepdims=True))
            p = jnp.exp2(st - m_new).astype(BF16)
            acc_ref[hh, :, right] = (acc_ref[hh, :, right] * jnp.exp2(m_old - m_new)
                                     + _dot(vt_ref[0, hh, :, pl.ds(k_last, tk)], p))
        for hh in range(nh):
            acc = acc_ref[hh]
            o = acc[:HEAD_DIM] / acc[HEAD_DIM:HEAD_DIM + 1]
            o_ref[0, hh * HEAD_DIM:(hh + 1) * HEAD_DIM, pl.ds(q0, tq)] = o.astype(o_ref.dtype)
        return carry

    lax.fori_loop(0, s // tq, q_tile, 0)


def _attention(q_aug, k_aug, vt_aug):
    b, n_heads, s, _ = q_aug.shape
    nh = ATTN_HEADS_PER_STEP
    tq, tk = ATTN_Q_TILE, ATTN_K_TILE
    assert s % tq == 0 and tq == 2 * tk and n_heads % nh == 0
    rows = pl.BlockSpec((1, nh, s, LANES), lambda bi, hi: (bi, hi, 0, 0))
    return pl.pallas_call(
        _attn_kernel,
        grid=(b, n_heads // nh),
        in_specs=[rows, rows, pl.BlockSpec((1, nh, V_ROWS, s), lambda bi, hi: (bi, hi, 0, 0))],
        out_specs=pl.BlockSpec((1, nh * HEAD_DIM, s), lambda bi, hi: (bi, hi, 0)),
        out_shape=jax.ShapeDtypeStruct((b, n_heads * HEAD_DIM, s), BF16),
        scratch_shapes=[pltpu.VMEM((nh, ATTN_SLOTS, tk, tq), F32), pltpu.VMEM((nh, V_ROWS, tq), F32)],
        compiler_params=pltpu.CompilerParams(
            dimension_semantics=("arbitrary", "arbitrary"), vmem_limit_bytes=48 * 1024 * 1024),
        name="fox_attention",
    )(q_aug, k_aug, vt_aug)


def _post_kernel(x_ref, mixc_ref, yt_ref, gattn_ref, wo_ref, gffn_ref, wup_ref, wfc_ref, wdown_ref,
                 gfin_ref, out_ref, ucarry_ref, h2_ref, up_ref, act_ref, *, d_ff, final_norm):
    tm = x_ref.shape[1]

    @pl.when(pl.program_id(1) == 0)
    def _():
        ucarry_ref[...] = jnp.zeros_like(ucarry_ref)

    yt = yt_ref[0].astype(F32)
    msa = jnp.mean(yt * yt, axis=0, keepdims=True)
    yn = (yt * lax.rsqrt(msa + EPS) * gattn_ref[...]).T.astype(BF16)
    mix = jnp.concatenate([mixc_ref[0], yn], axis=1)
    x1 = x_ref[0] + _dot(mix, wo_ref[...])

    ms = jnp.mean(x1 * x1, axis=-1, keepdims=True)
    h2 = (x1 * lax.rsqrt(ms + EPS) * gffn_ref[...]).astype(BF16)

    out_ref[0] = x1
    h2_ref[...] = h2
    starts = list(range(0, d_ff, FF_CHUNK))
    n_chunks = len(starts)
    width = lambda c: min(FF_CHUNK, d_ff - starts[c])
    cols = lambda c: (slice(starts[c], starts[c] + width(c)),
                      slice(d_ff + starts[c], d_ff + starts[c] + width(c)))

    def issue_up(c):
        w = width(c)
        for half, cc in enumerate(cols(c)):
            up_ref[c % FF_UP_SLOTS, half, :SUBLANES, :w] = ucarry_ref[:, cc]
            up_ref[c % FF_UP_SLOTS, half, SUBLANES:, :w] = _dot(h2_ref[...], wup_ref[:, cc])

    def gate_and_down(c):
        ca, cg = cols(c)
        w = width(c)
        slot = c % FF_UP_SLOTS
        for r0 in range(0, tm, FF_ROW_BLOCK):
            ext = slice(r0, r0 + FF_ROW_BLOCK + SUBLANES)
            a = _causal_conv3_ext(up_ref[slot, 0, ext, :w], wfc_ref[:, ca])
            g = _causal_conv3_ext(up_ref[slot, 1, ext, :w], wfc_ref[:, cg])
            act_ref[c % 2, r0:r0 + FF_ROW_BLOCK, :w] = (g * jax.nn.sigmoid(g) * a).astype(BF16)
        ucarry_ref[:, ca] = up_ref[slot, 0, tm:, :w]
        ucarry_ref[:, cg] = up_ref[slot, 1, tm:, :w]
        out_ref[0] += _dot(act_ref[c % 2, :, :w], wdown_ref[ca, :])

    for c in range(min(FF_UP_SLOTS - 1, n_chunks)):
        issue_up(c)
    for c in range(n_chunks):
        if c + FF_UP_SLOTS - 1 < n_chunks:
            issue_up(c + FF_UP_SLOTS - 1)
        gate_and_down(c)

    if final_norm:
        acc = out_ref[0]
        msf = jnp.mean(acc * acc, axis=-1, keepdims=True)
        out_ref[0] = acc * lax.rsqrt(msf + EPS) * gfin_ref[...]


def _post(x, mixc, yt, g_attn_out, w_o, g_ffn, w_up, w_ffn_conv, w_down, g_final, *, final_norm):
    b, s, d = x.shape
    conv_ch = mixc.shape[-1]
    attn_w = yt.shape[1]
    d_ff = w_down.shape[0]
    tm = ROW_TILE_POST
    assert s % tm == 0 and d_ff % LANES == 0 and FF_CHUNK % LANES == 0
    wfc = jnp.pad(w_ffn_conv, ((0, SUBLANES - CONV_K), (0, 0)))

    const = lambda shape: pl.BlockSpec(shape, lambda bi, i: (0,) * len(shape),
                                       pipeline_mode=pl.Buffered(1))
    return pl.pallas_call(
        functools.partial(_post_kernel, d_ff=d_ff, final_norm=final_norm),
        grid=(b, s // tm),
        in_specs=[
            pl.BlockSpec((1, tm, d), lambda bi, i: (bi, i, 0)),
            pl.BlockSpec((1, tm, conv_ch), lambda bi, i: (bi, i, 0)),
            pl.BlockSpec((1, attn_w, tm), lambda bi, i: (bi, 0, i)),
            const((attn_w, 1)), const(w_o.shape), const((1, d)), const(w_up.shape),
            const(wfc.shape), const(w_down.shape), const((1, d)),
        ],
        out_specs=pl.BlockSpec((1, tm, d), lambda bi, i: (bi, i, 0)),
        out_shape=jax.ShapeDtypeStruct((b, s, d), F32),
        scratch_shapes=[pltpu.VMEM((SUBLANES, 2 * d_ff), F32), pltpu.VMEM((tm, d), BF16),
                        pltpu.VMEM((FF_UP_SLOTS, 2, tm + SUBLANES, FF_CHUNK), F32),
                        pltpu.VMEM((2, tm, FF_CHUNK), BF16)],
        compiler_params=pltpu.CompilerParams(
            dimension_semantics=("arbitrary", "arbitrary"), vmem_limit_bytes=56 * 1024 * 1024),
        name="post_ffn",
    )(x, mixc, yt, g_attn_out.reshape(attn_w, 1), w_o.astype(BF16), g_ffn.reshape(1, d),
      w_up.astype(BF16), wfc, w_down.astype(BF16), g_final.reshape(1, d))


def kernel(x, g_mix, w_in, b_f, w_conv, g_conv_out, g_attn_out, w_o, g_ffn, w_up, w_ffn_conv, w_down,
           g_final):
    depth = w_in.shape[0]
    conv_ch = w_conv.shape[-1]
    n_heads = b_f.shape[-1]
    for l in range(depth):
        mixc, q_aug, k_aug, vt_aug = _inproj(x, g_mix[l], w_in[l], b_f[l], w_conv[l], g_conv_out[l],
                                             n_heads=n_heads, conv_ch=conv_ch)
        yt = _attention(q_aug, k_aug, vt_aug)
        x = _post(x, mixc, yt, g_attn_out[l], w_o[l], g_ffn[l], w_up[l], w_ffn_conv[l], w_down[l],
                  g_final, final_norm=(l == depth - 1))
    return x
```

```python
import functools

import jax
import jax.numpy as jnp
from jax import lax
from jax.experimental import pallas as pl
from jax.experimental.pallas import tpu as pltpu

EPS = 1e-6
CONV_K = 3
HEAD_DIM = 64
LANES = 128
SUBLANES = 8
NEG_BIG = -1e30
LOG2_E = 1.4426950408889634
V_ROWS = HEAD_DIM + 16

ROW_TILE_IN = 512
ROW_TILE_POST = 512
ATTN_Q_TILE = 512
ATTN_K_TILE = 256
ATTN_HEADS_PER_STEP = 2
ATTN_SLOTS = 2
FF_CHUNK = 1024
FF_UP_SLOTS = 3
FF_ROW_BLOCK = 64

BF16 = jnp.bfloat16
F32 = jnp.float32


def _dot(a, b):
    return jnp.dot(a, b, preferred_element_type=F32)


def _dot_nt(a, b):
    return lax.dot_general(a, b, (((1,), (1,)), ((), ())), preferred_element_type=F32)


def _split3(v):
    hi = v.astype(BF16).astype(F32)
    r1 = v - hi
    mid = r1.astype(BF16).astype(F32)
    lo = r1 - mid
    return hi, mid, lo


def _shift_rows(cur, prev8, k):
    ext = jnp.concatenate([prev8, cur], axis=0)
    return pltpu.roll(ext, k, axis=0)[SUBLANES:]


def _causal_conv3(cur, prev8, w):
    return (_shift_rows(cur, prev8, 2) * w[0:1] + _shift_rows(cur, prev8, 1) * w[1:2]
            + cur * w[2:3])


def _causal_conv3_ext(ext, w):
    return (pltpu.roll(ext, 2, axis=0)[SUBLANES:] * w[0:1] + pltpu.roll(ext, 1, axis=0)[SUBLANES:] * w[1:2]
            + ext[SUBLANES:] * w[2:3])


def _inproj_kernel(x_ref, gmix_ref, wg_ref, wqk_ref, wvt_ref, bft_ref, sel_ref, m0_ref, m1_ref,
                   wconv_ref, gconv_ref, mixc_ref, q_ref, k_ref, vt_ref, pcarry_ref, fcarry_ref,
                   *, n_heads, conv_ch):
    tm = x_ref.shape[1]
    attn_w = n_heads * HEAD_DIM
    gate_rows = fcarry_ref.shape[0]

    @pl.when(pl.program_id(1) == 0)
    def _():
        pcarry_ref[...] = jnp.zeros_like(pcarry_ref)
        fcarry_ref[...] = jnp.zeros_like(fcarry_ref)

    x = x_ref[0]
    ms = jnp.mean(x * x, axis=-1, keepdims=True)
    h = (x * lax.rsqrt(ms + EPS) * gmix_ref[...]).astype(BF16)

    vf = _dot_nt(wvt_ref[...], h)
    ones_rows = jnp.where(lax.broadcasted_iota(jnp.int32, (V_ROWS - HEAD_DIM, tm), 0) == 0, 1.0, 0.0)
    for hd in range(n_heads):
        vt_ref[0, hd] = jnp.concatenate(
            [vf[hd * HEAD_DIM:(hd + 1) * HEAD_DIM], ones_rows], axis=0).astype(BF16)

    zf = vf[attn_w:attn_w + gate_rows] + bft_ref[...]
    c = jnp.minimum(zf, 0.0) - jnp.log1p(jnp.exp(-jnp.abs(zf)))
    pos = lax.broadcasted_iota(jnp.int32, (gate_rows, tm), 1)
    d = 1
    while d < tm:
        c = c + jnp.where(pos >= d, pltpu.roll(c, d, axis=1), 0.0)
        d *= 2
    fcum = c + fcarry_ref[:, :1]
    fcarry_ref[...] = jnp.broadcast_to(fcum[:, tm - 1:tm], fcarry_ref.shape)

    f_hi, f_mid, f_lo = _split3(fcum * LOG2_E)
    pieces = jnp.concatenate(
        [f_hi, f_mid, f_lo, jnp.zeros((LANES - 3 * gate_rows, tm), F32)], axis=0).astype(BF16)

    zg = _dot(h, wg_ref[...])
    gb = zg[:, :conv_ch]
    p = zg[:, conv_ch:2 * conv_ch] * zg[:, 2 * conv_ch:]
    yc = gb * _causal_conv3(p, pcarry_ref[...], wconv_ref[...])
    pcarry_ref[...] = p[tm - SUBLANES:]
    msc = jnp.mean(yc * yc, axis=-1, keepdims=True)
    mixc_ref[0] = (yc * lax.rsqrt(msc + EPS) * gconv_ref[...]).astype(BF16)

    routed = _dot(sel_ref[...], pieces)
    extras = jnp.concatenate([jnp.zeros((HEAD_DIM, tm), F32), routed], axis=0).T
    lane = lax.broadcasted_iota(jnp.int32, (tm, LANES), 1)
    scale = LOG2_E / (HEAD_DIM ** 0.5)
    for pair in range(n_heads // 2):
        zp = _dot(h, wqk_ref[:, 2 * pair * LANES:2 * (pair + 1) * LANES])
        for hd in (2 * pair, 2 * pair + 1):
            qb = zp[:, :LANES] * scale
            kb = zp[:, LANES:]
            if hd % 2:
                qb = pltpu.roll(qb, HEAD_DIM, axis=1)
                kb = pltpu.roll(kb, HEAD_DIM, axis=1)
            m0 = m0_ref[hd:hd + 1, :]
            m1 = m1_ref[hd:hd + 1, :]
            q_ref[0, hd] = jnp.where(lane < HEAD_DIM, qb, extras * m0 + m1).astype(BF16)
            k_ref[0, hd] = jnp.where(lane < HEAD_DIM, kb, extras * m1 + m0).astype(BF16)


def _inproj(x, g_mix, w_in, b_f, w_conv, g_conv_out, *, n_heads, conv_ch):
    b, s, d = x.shape
    attn_w = n_heads * HEAD_DIM
    tm = ROW_TILE_IN
    gate_rows = -(-n_heads // SUBLANES) * SUBLANES
    extra = LANES - HEAD_DIM
    assert s % tm == 0 and n_heads % 2 == 0 and 6 * n_heads <= extra and 3 * gate_rows <= LANES
    o3 = 3 * conv_ch
    wg = w_in[:, :o3].astype(BF16)
    wqk = jnp.concatenate(
        [w_in[:, o3:o3 + attn_w].reshape(d, n_heads // 2, LANES),
         w_in[:, o3 + attn_w:o3 + 2 * attn_w].reshape(d, n_heads // 2, LANES)], axis=2,
    ).reshape(d, 2 * attn_w).astype(BF16)
    wvt = jnp.pad(w_in[:, o3 + 2 * attn_w:].T, ((0, 2 * SUBLANES - n_heads), (0, 0))).astype(BF16)
    bft = jnp.broadcast_to(jnp.pad(b_f, (0, gate_rows - n_heads))[:, None], (gate_rows, tm))
    wconv = jnp.pad(w_conv, ((0, SUBLANES - CONV_K), (0, 0)))
    hh = jnp.arange(n_heads)
    sel = jnp.zeros((extra, LANES), F32)
    m0 = jnp.zeros((n_heads, LANES), F32)
    m1 = jnp.zeros((n_heads, LANES), F32)
    for j in range(3):
        sel = sel.at[6 * hh + j, j * gate_rows + hh].set(1.0).at[6 * hh + 3 + j, j * gate_rows + hh].set(-1.0)
        m0 = m0.at[hh, HEAD_DIM + 6 * hh + j].set(1.0)
        m1 = m1.at[hh, HEAD_DIM + 6 * hh + 3 + j].set(1.0)

    const = lambda shape: pl.BlockSpec(shape, lambda bi, i: (0,) * len(shape),
                                       pipeline_mode=pl.Buffered(1))
    head_rows = pl.BlockSpec((1, n_heads, tm, LANES), lambda bi, i: (bi, 0, i, 0))
    return pl.pallas_call(
        functools.partial(_inproj_kernel, n_heads=n_heads, conv_ch=conv_ch),
        grid=(b, s // tm),
        in_specs=[
            pl.BlockSpec((1, tm, d), lambda bi, i: (bi, i, 0)),
            const((1, d)), const(wg.shape), const(wqk.shape), const(wvt.shape), const(bft.shape),
            const(sel.shape), const(m0.shape), const(m1.shape), const(wconv.shape), const((1, conv_ch)),
        ],
        out_specs=[
            pl.BlockSpec((1, tm, conv_ch), lambda bi, i: (bi, i, 0)),
            head_rows, head_rows,
            pl.BlockSpec((1, n_heads, V_ROWS, tm), lambda bi, i: (bi, 0, 0, i)),
        ],
        out_shape=[
            jax.ShapeDtypeStruct((b, s, conv_ch), BF16),
            jax.ShapeDtypeStruct((b, n_heads, s, LANES), BF16),
            jax.ShapeDtypeStruct((b, n_heads, s, LANES), BF16),
            jax.ShapeDtypeStruct((b, n_heads, V_ROWS, s), BF16),
        ],
        scratch_shapes=[pltpu.VMEM((SUBLANES, conv_ch), F32), pltpu.VMEM((gate_rows, LANES), F32)],
        compiler_params=pltpu.CompilerParams(
            dimension_semantics=("arbitrary", "arbitrary"), vmem_limit_bytes=48 * 1024 * 1024),
        name="inproj",
    )(x, g_mix.reshape(1, d), wg, wqk, wvt, bft, sel.astype(BF16), m0, m1, wconv,
      g_conv_out.reshape(1, conv_ch))


def _attn_kernel(q_ref, k_ref, vt_ref, o_ref, s_ref, acc_ref):
    nh = q_ref.shape[1]
    s = q_ref.shape[2]
    tq, tk = ATTN_Q_TILE, ATTN_K_TILE
    kpos = lax.broadcasted_iota(jnp.int32, (tk, tq), 0)
    qpos = lax.broadcasted_iota(jnp.int32, (tk, tq), 1)
    causal_sq = (lax.broadcasted_iota(jnp.int32, (tk, tq - tk), 0)
                 <= lax.broadcasted_iota(jnp.int32, (tk, tq - tk), 1))

    def load_q(qi):
        q0 = pl.multiple_of(qi * tq, tq)
        return [q_ref[0, hh, pl.ds(q0, tq), :] for hh in range(nh)]

    def issue_scores(j, slot, qs):
        k0 = pl.multiple_of(j * tk, tk)
        maxima = []
        for hh in range(nh):
            st = _dot_nt(k_ref[0, hh, pl.ds(k0, tk), :], qs[hh])
            s_ref[hh, slot] = st
            maxima.append(jnp.max(st, axis=0, keepdims=True))
        return tuple(maxima)

    def q_tile(qi, first_max):
        q0 = pl.multiple_of(qi * tq, tq)
        qs = load_q(qi)

        def consume(j, slot, ms, tile_max=None, key_offset=None):
            k0 = pl.multiple_of(j * tk, tk)
            out = []
            for hh in range(nh):
                def tile():
                    st = s_ref[hh, slot]
                    if key_offset is None:
                        return st
                    return jnp.where(kpos + key_offset <= qpos, st, NEG_BIG)
                tmax = tile_max[hh] if key_offset is None else jnp.max(tile(), axis=0, keepdims=True)
                m_new = jnp.maximum(ms[hh], tmax)
                p = jnp.exp2(tile() - m_new).astype(BF16)
                alpha = jnp.exp2(ms[hh] - m_new)
                acc_ref[hh] = acc_ref[hh] * alpha + _dot(vt_ref[0, hh, :, pl.ds(k0, tk)], p)
                out.append(m_new)
            return tuple(out)

        def kv_pair(jj, state):
            ms, max0 = state
            max1 = issue_scores(2 * jj + 1, 1, qs)
            ms = consume(2 * jj, 0, ms, max0)
            max0 = issue_scores(2 * jj + 2, 0, qs)
            return consume(2 * jj + 1, 1, ms, max1), max0

        acc_ref[...] = jnp.zeros_like(acc_ref)
        state = (tuple(jnp.full((1, tq), NEG_BIG, F32) for _ in range(nh)), first_max)
        def pairs(j0, n, st):
            for u in range(n):
                st = kv_pair(j0 + u, st)
            return st

        n1 = qi % 2
        n2 = 2 * ((qi // 2) % 2)
        state = lax.cond(n1 == 1, lambda st: pairs(0, 1, st), lambda st: st, state)
        state = lax.cond(n2 == 2, lambda st: pairs(n1, 2, st), lambda st: st, state)
        ms, _ = lax.fori_loop(0, qi // 4, lambda i, st: pairs(n1 + n2 + 4 * i, 4, st), state)
        right = slice(tk, tq)
        k_last = pl.multiple_of((2 * qi + 1) * tk, tk)
        for hh in range(nh):
            s_ref[hh, 1, :, right] = _dot_nt(k_ref[0, hh, pl.ds(k_last, tk), :], qs[hh][tk:])
        ms = consume(2 * qi, 0, ms, key_offset=0)
        next_max = issue_scores(0, 0, load_q(jnp.minimum(qi + 1, s // tq - 1)))
        for hh in range(nh):
            st = jnp.where(causal_sq, s_ref[hh, 1, :, right], NEG_BIG)
            m_old = ms[hh][:, right]
            m_new = jnp.maximum(m_old, jnp.max(st, axis=0, keepdims=True))
            p = jnp.exp2(st - m_new).astype(BF16)
            acc_ref[hh, :, right] = (acc_ref[hh, :, right] * jnp.exp2(m_old - m_new)
                                     + _dot(vt_ref[0, hh, :, pl.ds(k_last, tk)], p))
        for hh in range(nh):
            acc = acc_ref[hh]
            o = acc[:HEAD_DIM] / acc[HEAD_DIM:HEAD_DIM + 1]
            o_ref[0, hh * HEAD_DIM:(hh + 1) * HEAD_DIM, pl.ds(q0, tq)] = o.astype(o_ref.dtype)
        return next_max

    lax.fori_loop(0, s // tq, q_tile, issue_scores(0, 0, load_q(0)))


def _attention(q_aug, k_aug, vt_aug):
    b, n_heads, s, _ = q_aug.shape
    nh = ATTN_HEADS_PER_STEP
    tq, tk = ATTN_Q_TILE, ATTN_K_TILE
    assert s % tq == 0 and tq == 2 * tk and n_heads % nh == 0
    rows = pl.BlockSpec((1, nh, s, LANES), lambda bi, hi: (bi, hi, 0, 0))
    return pl.pallas_call(
        _attn_kernel,
        grid=(b, n_heads // nh),
        in_specs=[rows, rows, pl.BlockSpec((1, nh, V_ROWS, s), lambda bi, hi: (bi, hi, 0, 0))],
        out_specs=pl.BlockSpec((1, nh * HEAD_DIM, s), lambda bi, hi: (bi, hi, 0)),
        out_shape=jax.ShapeDtypeStruct((b, n_heads * HEAD_DIM, s), BF16),
        scratch_shapes=[pltpu.VMEM((nh, ATTN_SLOTS, tk, tq), F32), pltpu.VMEM((nh, V_ROWS, tq), F32)],
        compiler_params=pltpu.CompilerParams(
            dimension_semantics=("arbitrary", "arbitrary"), vmem_limit_bytes=48 * 1024 * 1024),
        name="fox_attention",
    )(q_aug, k_aug, vt_aug)


def _post_kernel(x_ref, mixc_ref, yt_ref, gattn_ref, wo_ref, gffn_ref, wup_ref, wfc_ref, wdown_ref,
                 gfin_ref, out_ref, ucarry_ref, h2_ref, up_ref, act_ref, *, d_ff, final_norm):
    tm = x_ref.shape[1]

    @pl.when(pl.program_id(1) == 0)
    def _():
        ucarry_ref[...] = jnp.zeros_like(ucarry_ref)

    yt = yt_ref[0].astype(F32)
    msa = jnp.mean(yt * yt, axis=0, keepdims=True)
    yn = (yt * lax.rsqrt(msa + EPS) * gattn_ref[...]).T.astype(BF16)
    mix = jnp.concatenate([mixc_ref[0], yn], axis=1)
    x1 = x_ref[0] + _dot(mix, wo_ref[...])

    ms = jnp.mean(x1 * x1, axis=-1, keepdims=True)
    h2 = (x1 * lax.rsqrt(ms + EPS) * gffn_ref[...]).astype(BF16)

    out_ref[0] = x1
    h2_ref[...] = h2
    starts = list(range(0, d_ff, FF_CHUNK))
    n_chunks = len(starts)
    width = lambda c: min(FF_CHUNK, d_ff - starts[c])
    cols = lambda c: (slice(starts[c], starts[c] + width(c)),
                      slice(d_ff + starts[c], d_ff + starts[c] + width(c)))

    def issue_up(c):
        w = width(c)
        for half, cc in enumerate(cols(c)):
            up_ref[c % FF_UP_SLOTS, half, :SUBLANES, :w] = ucarry_ref[:, cc]
            up_ref[c % FF_UP_SLOTS, half, SUBLANES:, :w] = _dot(h2_ref[...], wup_ref[:, cc])

    def gate_and_down(c):
        ca, cg = cols(c)
        w = width(c)
        slot = c % FF_UP_SLOTS
        for r0 in range(0, tm, FF_ROW_BLOCK):
            ext = slice(r0, r0 + FF_ROW_BLOCK + SUBLANES)
            a = _causal_conv3_ext(up_ref[slot, 0, ext, :w], wfc_ref[:, ca])
            g = _causal_conv3_ext(up_ref[slot, 1, ext, :w], wfc_ref[:, cg])
            act_ref[c % 2, r0:r0 + FF_ROW_BLOCK, :w] = (g * jax.nn.sigmoid(g) * a).astype(BF16)
        ucarry_ref[:, ca] = up_ref[slot, 0, tm:, :w]
        ucarry_ref[:, cg] = up_ref[slot, 1, tm:, :w]
        out_ref[0] += _dot(act_ref[c % 2, :, :w], wdown_ref[ca, :])

    for c in range(min(FF_UP_SLOTS - 1, n_chunks)):
        issue_up(c)
    for c in range(n_chunks):
        if c + FF_UP_SLOTS - 1 < n_chunks:
            issue_up(c + FF_UP_SLOTS - 1)
        gate_and_down(c)

    if final_norm:
        acc = out_ref[0]
        msf = jnp.mean(acc * acc, axis=-1, keepdims=True)
        out_ref[0] = acc * lax.rsqrt(msf + EPS) * gfin_ref[...]


def _post(x, mixc, yt, g_attn_out, w_o, g_ffn, w_up, w_ffn_conv, w_down, g_final, *, final_norm):
    b, s, d = x.shape
    conv_ch = mixc.shape[-1]
    attn_w = yt.shape[1]
    d_ff = w_down.shape[0]
    tm = ROW_TILE_POST
    assert s % tm == 0 and d_ff % LANES == 0 and FF_CHUNK % LANES == 0
    wfc = jnp.pad(w_ffn_conv, ((0, SUBLANES - CONV_K), (0, 0)))

    const = lambda shape: pl.BlockSpec(shape, lambda bi, i: (0,) * len(shape),
                                       pipeline_mode=pl.Buffered(1))
    return pl.pallas_call(
        functools.partial(_post_kernel, d_ff=d_ff, final_norm=final_norm),
        grid=(b, s // tm),
        in_specs=[
            pl.BlockSpec((1, tm, d), lambda bi, i: (bi, i, 0)),
            pl.BlockSpec((1, tm, conv_ch), lambda bi, i: (bi, i, 0)),
            pl.BlockSpec((1, attn_w, tm), lambda bi, i: (bi, 0, i)),
            const((attn_w, 1)), const(w_o.shape), const((1, d)), const(w_up.shape),
            const(wfc.shape), const(w_down.shape), const((1, d)),
        ],
        out_specs=pl.BlockSpec((1, tm, d), lambda bi, i: (bi, i, 0)),
        out_shape=jax.ShapeDtypeStruct((b, s, d), F32),
        scratch_shapes=[pltpu.VMEM((SUBLANES, 2 * d_ff), F32), pltpu.VMEM((tm, d), BF16),
                        pltpu.VMEM((FF_UP_SLOTS, 2, tm + SUBLANES, FF_CHUNK), F32),
                        pltpu.VMEM((2, tm, FF_CHUNK), BF16)],
        compiler_params=pltpu.CompilerParams(
            dimension_semantics=("arbitrary", "arbitrary"), vmem_limit_bytes=56 * 1024 * 1024),
        name="post_ffn",
    )(x, mixc, yt, g_attn_out.reshape(attn_w, 1), w_o.astype(BF16), g_ffn.reshape(1, d),
      w_up.astype(BF16), wfc, w_down.astype(BF16), g_final.reshape(1, d))


def kernel(x, g_mix, w_in, b_f, w_conv, g_conv_out, g_attn_out, w_o, g_ffn, w_up, w_ffn_conv, w_down,
           g_final):
    depth = w_in.shape[0]
    conv_ch = w_conv.shape[-1]
    n_heads = b_f.shape[-1]
    for l in range(depth):
        mixc, q_aug, k_aug, vt_aug = _inproj(x, g_mix[l], w_in[l], b_f[l], w_conv[l], g_conv_out[l],
                                             n_heads=n_heads, conv_ch=conv_ch)
        yt = _attention(q_aug, k_aug, vt_aug)
        x = _post(x, mixc, yt, g_attn_out[l], w_o[l], g_ffn[l], w_up[l], w_ffn_conv[l], w_down[l],
                  g_final, final_norm=(l == depth - 1))
    return x
```

```python
import functools

import jax
import jax.numpy as jnp
from jax import lax
from jax.experimental import pallas as pl
from jax.experimental.pallas import tpu as pltpu

EPS = 1e-6
CONV_K = 3
HEAD_DIM = 64
LANES = 128
SUBLANES = 8
NEG_BIG = -1e30
LOG2_E = 1.4426950408889634
V_ROWS = HEAD_DIM + 16

ROW_TILE_IN = 1024
ROW_TILE_POST = 512
ATTN_Q_TILE = 512
ATTN_K_TILE = 256
ATTN_HEADS_PER_STEP = 2
ATTN_SLOTS = 2
FF_CHUNK = 1024
FF_UP_SLOTS = 3
FF_ROW_BLOCK = 64

BF16 = jnp.bfloat16
F32 = jnp.float32


def _dot(a, b):
    return jnp.dot(a, b, preferred_element_type=F32)


def _dot_nt(a, b):
    return lax.dot_general(a, b, (((1,), (1,)), ((), ())), preferred_element_type=F32)


def _split3(v):
    hi = v.astype(BF16).astype(F32)
    r1 = v - hi
    mid = r1.astype(BF16).astype(F32)
    lo = r1 - mid
    return hi, mid, lo


def _shift_rows(cur, prev8, k):
    ext = jnp.concatenate([prev8, cur], axis=0)
    return pltpu.roll(ext, k, axis=0)[SUBLANES:]


def _causal_conv3(cur, prev8, w):
    return (_shift_rows(cur, prev8, 2) * w[0:1] + _shift_rows(cur, prev8, 1) * w[1:2]
            + cur * w[2:3])


def _causal_conv3_ext(ext, w):
    return (pltpu.roll(ext, 2, axis=0)[SUBLANES:] * w[0:1] + pltpu.roll(ext, 1, axis=0)[SUBLANES:] * w[1:2]
            + ext[SUBLANES:] * w[2:3])


def _inproj_kernel(x_ref, gmix_ref, wg_ref, wk_ref, wvt_ref, bft_ref, sel_ref, m0_ref, m1_ref,
                   wconv_ref, gconv_ref, mixc_ref, qt_ref, k_ref, vt_ref, pcarry_ref, fcarry_ref,
                   *, n_heads, conv_ch):
    tm = x_ref.shape[1]
    attn_w = n_heads * HEAD_DIM
    gate_rows = fcarry_ref.shape[0]

    @pl.when(pl.program_id(1) == 0)
    def _():
        pcarry_ref[...] = jnp.zeros_like(pcarry_ref)
        fcarry_ref[...] = jnp.zeros_like(fcarry_ref)

    x = x_ref[0]
    ms = jnp.mean(x * x, axis=-1, keepdims=True)
    h = (x * lax.rsqrt(ms + EPS) * gmix_ref[...]).astype(BF16)

    vf = _dot_nt(wvt_ref[...], h)
    ones_rows = jnp.where(lax.broadcasted_iota(jnp.int32, (V_ROWS - HEAD_DIM, tm), 0) == 0, 1.0, 0.0)
    for hd in range(n_heads):
        vt_ref[0, hd] = jnp.concatenate(
            [vf[hd * HEAD_DIM:(hd + 1) * HEAD_DIM], ones_rows], axis=0).astype(BF16)

    zf = vf[2 * attn_w:2 * attn_w + gate_rows] + bft_ref[...]
    c = jnp.minimum(zf, 0.0) - jnp.log1p(jnp.exp(-jnp.abs(zf)))
    pos = lax.broadcasted_iota(jnp.int32, (gate_rows, tm), 1)
    d = 1
    while d < tm:
        c = c + jnp.where(pos >= d, pltpu.roll(c, d, axis=1), 0.0)
        d *= 2
    fcum = c + fcarry_ref[:, :1]
    fcarry_ref[...] = jnp.broadcast_to(fcum[:, tm - 1:tm], fcarry_ref.shape)

    f_hi, f_mid, f_lo = _split3(fcum * LOG2_E)
    pieces = jnp.concatenate(
        [f_hi, f_mid, f_lo, jnp.zeros((LANES - 3 * gate_rows, tm), F32)], axis=0).astype(BF16)

    zg = _dot(h, wg_ref[...])
    gb = zg[:, :conv_ch]
    p = zg[:, conv_ch:2 * conv_ch] * zg[:, 2 * conv_ch:]
    yc = gb * _causal_conv3(p, pcarry_ref[...], wconv_ref[...])
    pcarry_ref[...] = p[tm - SUBLANES:]
    msc = jnp.mean(yc * yc, axis=-1, keepdims=True)
    mixc_ref[0] = (yc * lax.rsqrt(msc + EPS) * gconv_ref[...]).astype(BF16)

    routed = _dot(sel_ref[...], pieces)

    scale = LOG2_E / (HEAD_DIM ** 0.5)
    extra_row = lax.broadcasted_iota(jnp.int32, (LANES - HEAD_DIM, tm), 0)
    for hd in range(n_heads):
        q_extra = jnp.where((extra_row >= 6 * hd) & (extra_row < 6 * hd + 3), routed,
                            jnp.where((extra_row >= 6 * hd + 3) & (extra_row < 6 * hd + 6), 1.0, 0.0))
        qt_ref[0, hd] = jnp.concatenate(
            [vf[attn_w + hd * HEAD_DIM:attn_w + (hd + 1) * HEAD_DIM] * scale, q_extra], axis=0).astype(BF16)

    extras = jnp.concatenate([jnp.zeros((HEAD_DIM, tm), F32), routed], axis=0).T
    lane = lax.broadcasted_iota(jnp.int32, (tm, LANES), 1)
    for grp in range(n_heads // 4):
        zk = _dot(h, wk_ref[:, 2 * grp * LANES:2 * (grp + 1) * LANES])
        for hd in range(4 * grp, 4 * grp + 4):
            kb = zk[:, ((hd // 2) % 2) * LANES:((hd // 2) % 2 + 1) * LANES]
            if hd % 2:
                kb = pltpu.roll(kb, HEAD_DIM, axis=1)
            m0 = m0_ref[hd:hd + 1, :]
            m1 = m1_ref[hd:hd + 1, :]
            k_ref[0, hd] = jnp.where(lane < HEAD_DIM, kb, extras * m1 + m0).astype(BF16)


def _inproj(x, g_mix, w_in, b_f, w_conv, g_conv_out, *, n_heads, conv_ch):
    b, s, d = x.shape
    attn_w = n_heads * HEAD_DIM
    tm = ROW_TILE_IN
    gate_rows = -(-n_heads // SUBLANES) * SUBLANES
    extra = LANES - HEAD_DIM
    assert s % tm == 0 and n_heads % 4 == 0 and 6 * n_heads <= extra and 3 * gate_rows <= LANES
    o3 = 3 * conv_ch
    wg = w_in[:, :o3].astype(BF16)
    wk = w_in[:, o3 + attn_w:o3 + 2 * attn_w].astype(BF16)
    wvt = jnp.concatenate(
        [w_in[:, o3 + 2 * attn_w:o3 + 3 * attn_w].T, w_in[:, o3:o3 + attn_w].T,
         jnp.pad(w_in[:, o3 + 3 * attn_w:].T, ((0, 2 * SUBLANES - n_heads), (0, 0)))], axis=0).astype(BF16)
    bft = jnp.broadcast_to(jnp.pad(b_f, (0, gate_rows - n_heads))[:, None], (gate_rows, tm))
    wconv = jnp.pad(w_conv, ((0, SUBLANES - CONV_K), (0, 0)))
    hh = jnp.arange(n_heads)
    sel = jnp.zeros((extra, LANES), F32)
    m0 = jnp.zeros((n_heads, LANES), F32)
    m1 = jnp.zeros((n_heads, LANES), F32)
    for j in range(3):
        sel = sel.at[6 * hh + j, j * gate_rows + hh].set(1.0).at[6 * hh + 3 + j, j * gate_rows + hh].set(-1.0)
        m0 = m0.at[hh, HEAD_DIM + 6 * hh + j].set(1.0)
        m1 = m1.at[hh, HEAD_DIM + 6 * hh + 3 + j].set(1.0)

    const = lambda shape: pl.BlockSpec(shape, lambda bi, i: (0,) * len(shape),
                                       pipeline_mode=pl.Buffered(1))
    head_rows = pl.BlockSpec((1, n_heads, tm, LANES), lambda bi, i: (bi, 0, i, 0))
    return pl.pallas_call(
        functools.partial(_inproj_kernel, n_heads=n_heads, conv_ch=conv_ch),
        grid=(b, s // tm),
        in_specs=[
            pl.BlockSpec((1, tm, d), lambda bi, i: (bi, i, 0)),
            const((1, d)), const(wg.shape), const(wk.shape), const(wvt.shape), const(bft.shape),
            const(sel.shape), const(m0.shape), const(m1.shape), const(wconv.shape), const((1, conv_ch)),
        ],
        out_specs=[
            pl.BlockSpec((1, tm, conv_ch), lambda bi, i: (bi, i, 0)),
            pl.BlockSpec((1, n_heads, LANES, tm), lambda bi, i: (bi, 0, 0, i)),
            head_rows,
            pl.BlockSpec((1, n_heads, V_ROWS, tm), lambda bi, i: (bi, 0, 0, i)),
        ],
        out_shape=[
            jax.ShapeDtypeStruct((b, s, conv_ch), BF16),
            jax.ShapeDtypeStruct((b, n_heads, LANES, s), BF16),
            jax.ShapeDtypeStruct((b, n_heads, s, LANES), BF16),
            jax.ShapeDtypeStruct((b, n_heads, V_ROWS, s), BF16),
        ],
        scratch_shapes=[pltpu.VMEM((SUBLANES, conv_ch), F32), pltpu.VMEM((gate_rows, LANES), F32)],
        compiler_params=pltpu.CompilerParams(
            dimension_semantics=("arbitrary", "arbitrary"), vmem_limit_bytes=48 * 1024 * 1024),
        name="inproj",
    )(x, g_mix.reshape(1, d), wg, wk, wvt, bft, sel.astype(BF16), m0, m1, wconv,
      g_conv_out.reshape(1, conv_ch))


def _attn_kernel(qt_ref, k_ref, vt_ref, o_ref, s_ref, acc_ref):
    nh = k_ref.shape[1]
    s = k_ref.shape[2]
    tq, tk = ATTN_Q_TILE, ATTN_K_TILE
    r = tq // tk

    def causal(d):
        shape = (tk, tq - d * tk)
        return lax.broadcasted_iota(jnp.int32, shape, 0) <= lax.broadcasted_iota(jnp.int32, shape, 1)

    def load_q(qi):
        q0 = pl.multiple_of(qi * tq, tq)
        return [qt_ref[0, hh, :, pl.ds(q0, tq)] for hh in range(nh)]

    def issue_scores(j, slot, qs):
        k0 = pl.multiple_of(j * tk, tk)
        maxima = []
        for hh in range(nh):
            st = _dot(k_ref[0, hh, pl.ds(k0, tk), :], qs[hh])
            s_ref[hh, slot] = st
            maxima.append(jnp.max(st, axis=0, keepdims=True))
        return tuple(maxima)

    def q_tile(qi, first_max):
        q0 = pl.multiple_of(qi * tq, tq)
        qs = load_q(qi)

        def consume(j, slot, ms, tile_max):
            k0 = pl.multiple_of(j * tk, tk)
            out = []
            for hh in range(nh):
                m_new = jnp.maximum(ms[hh], tile_max[hh])
                p = jnp.exp2(s_ref[hh, slot] - m_new).astype(BF16)
                alpha = jnp.exp2(ms[hh] - m_new)
                acc_ref[hh] = acc_ref[hh] * alpha + _dot(vt_ref[0, hh, :, pl.ds(k0, tk)], p)
                out.append(m_new)
            return tuple(out)

        def kv_pair(jj, state):
            ms, max0 = state
            max1 = issue_scores(2 * jj + 1, 1, qs)
            ms = consume(2 * jj, 0, ms, max0)
            max0 = issue_scores(2 * jj + 2, 0, qs)
            return consume(2 * jj + 1, 1, ms, max1), max0

        acc_ref[...] = jnp.zeros_like(acc_ref)
        state = (tuple(jnp.full((1, tq), NEG_BIG, F32) for _ in range(nh)), first_max)
        def pairs(j0, n, st):
            for u in range(n):
                st = kv_pair(j0 + u, st)
            return st

        n_pairs = qi * (r // 2)
        n1 = n_pairs % 2
        n2 = 2 * ((n_pairs // 2) % 2)
        state = lax.cond(n1 == 1, lambda st: pairs(0, 1, st), lambda st: st, state)
        state = lax.cond(n2 == 2, lambda st: pairs(n1, 2, st), lambda st: st, state)
        ms, _ = lax.fori_loop(0, n_pairs // 4, lambda i, st: pairs(n1 + n2 + 4 * i, 4, st), state)

        def diag_cols(d):
            return slice(d * tk, tq), pl.multiple_of((r * qi + d) * tk, tk)

        def issue_diag(d):
            cols, k0 = diag_cols(d)
            for hh in range(nh):
                s_ref[hh, d % 2, :, cols] = _dot(k_ref[0, hh, pl.ds(k0, tk), :], qs[hh][:, d * tk:])

        def consume_diag(d, ms):
            cols, k0 = diag_cols(d)
            out = []
            for hh in range(nh):
                st = jnp.where(causal(d), s_ref[hh, d % 2, :, cols], NEG_BIG)
                m_old = ms[hh][:, cols]
                m_new = jnp.maximum(m_old, jnp.max(st, axis=0, keepdims=True))
                p = jnp.exp2(st - m_new).astype(BF16)
                acc_ref[hh, :, cols] = (acc_ref[hh, :, cols] * jnp.exp2(m_old - m_new)
                                        + _dot(vt_ref[0, hh, :, pl.ds(k0, tk)], p))
                out.append(m_new if d == 0 else jnp.concatenate([ms[hh][:, :d * tk], m_new], axis=1))
            return tuple(out)

        issue_diag(1)
        for d in range(r):
            ms = consume_diag(d, ms)
            if d + 2 < r:
                issue_diag(d + 2)
            elif d == r - 2:
                next_max = issue_scores(0, 0, load_q(jnp.minimum(qi + 1, s // tq - 1)))
        for hh in range(nh):
            acc = acc_ref[hh]
            o = acc[:HEAD_DIM] / acc[HEAD_DIM:HEAD_DIM + 1]
            o_ref[0, hh * HEAD_DIM:(hh + 1) * HEAD_DIM, pl.ds(q0, tq)] = o.astype(o_ref.dtype)
        return next_max

    lax.fori_loop(0, s // tq, q_tile, issue_scores(0, 0, load_q(0)))


def _attention(qt_aug, k_aug, vt_aug):
    b, n_heads, s, _ = k_aug.shape
    nh = ATTN_HEADS_PER_STEP
    tq, tk = ATTN_Q_TILE, ATTN_K_TILE
    assert s % tq == 0 and tq % (2 * tk) == 0 and n_heads % nh == 0
    rows = pl.BlockSpec((1, nh, s, LANES), lambda bi, hi: (bi, hi, 0, 0))
    return pl.pallas_call(
        _attn_kernel,
        grid=(b, n_heads // nh),
        in_specs=[pl.BlockSpec((1, nh, LANES, s), lambda bi, hi: (bi, hi, 0, 0)), rows,
                  pl.BlockSpec((1, nh, V_ROWS, s), lambda bi, hi: (bi, hi, 0, 0))],
        out_specs=pl.BlockSpec((1, nh * HEAD_DIM, s), lambda bi, hi: (bi, hi, 0)),
        out_shape=jax.ShapeDtypeStruct((b, n_heads * HEAD_DIM, s), BF16),
        scratch_shapes=[pltpu.VMEM((nh, ATTN_SLOTS, tk, tq), F32), pltpu.VMEM((nh, V_ROWS, tq), F32)],
        compiler_params=pltpu.CompilerParams(
            dimension_semantics=("arbitrary", "arbitrary"), vmem_limit_bytes=48 * 1024 * 1024),
        name="fox_attention",
    )(qt_aug, k_aug, vt_aug)


def _post_kernel(x_ref, mixc_ref, yt_ref, gattn_ref, wo_ref, gffn_ref, wup_ref, wfc_ref, wdown_ref,
                 gfin_ref, out_ref, ucarry_ref, h2_ref, up_ref, act_ref, *, d_ff, final_norm):
    tm = x_ref.shape[1]

    @pl.when(pl.program_id(1) == 0)
    def _():
        ucarry_ref[...] = jnp.zeros_like(ucarry_ref)

    yt = yt_ref[0].astype(F32)
    msa = jnp.mean(yt * yt, axis=0, keepdims=True)
    yn = (yt * lax.rsqrt(msa + EPS) * gattn_ref[...]).T.astype(BF16)
    mix = jnp.concatenate([mixc_ref[0], yn], axis=1)
    x1 = x_ref[0] + _dot(mix, wo_ref[...])

    ms = jnp.mean(x1 * x1, axis=-1, keepdims=True)
    h2 = (x1 * lax.rsqrt(ms + EPS) * gffn_ref[...]).astype(BF16)

    out_ref[0] = x1
    h2_ref[...] = h2
    starts = list(range(0, d_ff, FF_CHUNK))
    n_chunks = len(starts)
    width = lambda c: min(FF_CHUNK, d_ff - starts[c])
    cols = lambda c: (slice(starts[c], starts[c] + width(c)),
                      slice(d_ff + starts[c], d_ff + starts[c] + width(c)))

    def issue_up(c):
        w = width(c)
        for half, cc in enumerate(cols(c)):
            up_ref[c % FF_UP_SLOTS, half, :SUBLANES, :w] = ucarry_ref[:, cc]
            up_ref[c % FF_UP_SLOTS, half, SUBLANES:, :w] = _dot(h2_ref[...], wup_ref[:, cc])

    def gate_and_down(c):
        ca, cg = cols(c)
        w = width(c)
        slot = c % FF_UP_SLOTS
        for r0 in range(0, tm, FF_ROW_BLOCK):
            ext = slice(r0, r0 + FF_ROW_BLOCK + SUBLANES)
            a = _causal_conv3_ext(up_ref[slot, 0, ext, :w], wfc_ref[:, ca])
            g = _causal_conv3_ext(up_ref[slot, 1, ext, :w], wfc_ref[:, cg])
            act_ref[c % 2, r0:r0 + FF_ROW_BLOCK, :w] = (g * jax.nn.sigmoid(g) * a).astype(BF16)
        ucarry_ref[:, ca] = up_ref[slot, 0, tm:, :w]
        ucarry_ref[:, cg] = up_ref[slot, 1, tm:, :w]
        out_ref[0] += _dot(act_ref[c % 2, :, :w], wdown_ref[ca, :])

    for c in range(min(FF_UP_SLOTS - 1, n_chunks)):
        issue_up(c)
    for c in range(n_chunks):
        if c + FF_UP_SLOTS - 1 < n_chunks:
            issue_up(c + FF_UP_SLOTS - 1)
        gate_and_down(c)

    if final_norm:
        acc = out_ref[0]
        msf = jnp.mean(acc * acc, axis=-1, keepdims=True)
        out_ref[0] = acc * lax.rsqrt(msf + EPS) * gfin_ref[...]


def _post(x, mixc, yt, g_attn_out, w_o, g_ffn, w_up, w_ffn_conv, w_down, g_final, *, final_norm):
    b, s, d = x.shape
    conv_ch = mixc.shape[-1]
    attn_w = yt.shape[1]
    d_ff = w_down.shape[0]
    tm = ROW_TILE_POST
    assert s % tm == 0 and d_ff % LANES == 0 and FF_CHUNK % LANES == 0
    wfc = jnp.pad(w_ffn_conv, ((0, SUBLANES - CONV_K), (0, 0)))

    const = lambda shape: pl.BlockSpec(shape, lambda bi, i: (0,) * len(shape),
                                       pipeline_mode=pl.Buffered(1))
    return pl.pallas_call(
        functools.partial(_post_kernel, d_ff=d_ff, final_norm=final_norm),
        grid=(b, s // tm),
        in_specs=[
            pl.BlockSpec((1, tm, d), lambda bi, i: (bi, i, 0)),
            pl.BlockSpec((1, tm, conv_ch), lambda bi, i: (bi, i, 0)),
            pl.BlockSpec((1, attn_w, tm), lambda bi, i: (bi, 0, i)),
            const((attn_w, 1)), const(w_o.shape), const((1, d)), const(w_up.shape),
            const(wfc.shape), const(w_down.shape), const((1, d)),
        ],
        out_specs=pl.BlockSpec((1, tm, d), lambda bi, i: (bi, i, 0)),
        out_shape=jax.ShapeDtypeStruct((b, s, d), F32),
        scratch_shapes=[pltpu.VMEM((SUBLANES, 2 * d_ff), F32), pltpu.VMEM((tm, d), BF16),
                        pltpu.VMEM((FF_UP_SLOTS, 2, tm + SUBLANES, FF_CHUNK), F32),
                        pltpu.VMEM((2, tm, FF_CHUNK), BF16)],
        compiler_params=pltpu.CompilerParams(
            dimension_semantics=("arbitrary", "arbitrary"), vmem_limit_bytes=56 * 1024 * 1024),
        name="post_ffn",
    )(x, mixc, yt, g_attn_out.reshape(attn_w, 1), w_o.astype(BF16), g_ffn.reshape(1, d),
      w_up.astype(BF16), wfc, w_down.astype(BF16), g_final.reshape(1, d))


def kernel(x, g_mix, w_in, b_f, w_conv, g_conv_out, g_attn_out, w_o, g_ffn, w_up, w_ffn_conv, w_down,
           g_final):
    depth = w_in.shape[0]
    conv_ch = w_conv.shape[-1]
    n_heads = b_f.shape[-1]
    for l in range(depth):
        mixc, qt_aug, k_aug, vt_aug = _inproj(x, g_mix[l], w_in[l], b_f[l], w_conv[l], g_conv_out[l],
                                             n_heads=n_heads, conv_ch=conv_ch)
        yt = _attention(qt_aug, k_aug, vt_aug)
        x = _post(x, mixc, yt, g_attn_out[l], w_o[l], g_ffn[l], w_up[l], w_ffn_conv[l], w_down[l],
                  g_final, final_norm=(l == depth - 1))
    return x
```

```python
import functools

import jax
import jax.numpy as jnp
import numpy as np
from jax import lax
from jax.experimental import pallas as pl
from jax.experimental.pallas import tpu as pltpu

EPS = 1e-6
CONV_K = 3
HEAD_DIM = 64
LANES = 128
SUBLANES = 8
NEG_BIG = -1e30
LOG2_E = 1.4426950408889634
V_ROWS = HEAD_DIM + 16

ROW_TILE_IN = 1024
ROW_TILE_POST = 512
ATTN_Q_TILE = 512
ATTN_K_TILE = 256
ATTN_HEADS_PER_STEP = 2
ATTN_SLOTS = 2
FF_CHUNK = 1024
FF_UP_SLOTS = 3
FF_ROW_BLOCK = 64

BF16 = jnp.bfloat16
F32 = jnp.float32

V7X_VMEM_BYTES = 64 * 1024 * 1024
VMEM_REQUEST_CAP = V7X_VMEM_BYTES * 7 // 8
VMEM_TEMPORARIES = 1.5


def _vmem_limit(*buffers):
    need = sum(count * int(np.prod(shape)) * jnp.dtype(dtype).itemsize for shape, dtype, count in buffers)
    return min(VMEM_REQUEST_CAP, int(need * VMEM_TEMPORARIES))


def _dot(a, b):
    return jnp.dot(a, b, preferred_element_type=F32)


def _dot_nt(a, b):
    return lax.dot_general(a, b, (((1,), (1,)), ((), ())), preferred_element_type=F32)


def _split3(v):
    hi = v.astype(BF16).astype(F32)
    r1 = v - hi
    mid = r1.astype(BF16).astype(F32)
    lo = r1 - mid
    return hi, mid, lo


def _shift_rows(cur, prev8, k):
    ext = jnp.concatenate([prev8, cur], axis=0)
    return pltpu.roll(ext, k, axis=0)[SUBLANES:]


def _causal_conv3(cur, prev8, w):
    return (_shift_rows(cur, prev8, 2) * w[0:1] + _shift_rows(cur, prev8, 1) * w[1:2]
            + cur * w[2:3])


def _causal_conv3_ext(ext, w):
    return (pltpu.roll(ext, 2, axis=0)[SUBLANES:] * w[0:1] + pltpu.roll(ext, 1, axis=0)[SUBLANES:] * w[1:2]
            + ext[SUBLANES:] * w[2:3])


def _inproj_kernel(x_ref, gmix_ref, wg_ref, wk_ref, wvt_ref, bft_ref, sel_ref, m0_ref, m1_ref,
                   wconv_ref, gconv_ref, mixc_ref, qt_ref, k_ref, vt_ref, pcarry_ref, fcarry_ref,
                   *, n_heads, conv_ch):
    tm = x_ref.shape[1]
    attn_w = n_heads * HEAD_DIM
    gate_rows = fcarry_ref.shape[0]

    @pl.when(pl.program_id(1) == 0)
    def _():
        pcarry_ref[...] = jnp.zeros_like(pcarry_ref)
        fcarry_ref[...] = jnp.zeros_like(fcarry_ref)

    x = x_ref[0]
    ms = jnp.mean(x * x, axis=-1, keepdims=True)
    h = (x * lax.rsqrt(ms + EPS) * gmix_ref[...]).astype(BF16)

    vf = _dot_nt(wvt_ref[...], h)
    ones_rows = jnp.where(lax.broadcasted_iota(jnp.int32, (V_ROWS - HEAD_DIM, tm), 0) == 0, 1.0, 0.0)
    for hd in range(n_heads):
        vt_ref[0, hd] = jnp.concatenate(
            [vf[hd * HEAD_DIM:(hd + 1) * HEAD_DIM], ones_rows], axis=0).astype(BF16)

    zf = vf[2 * attn_w:2 * attn_w + gate_rows] + bft_ref[...]
    c = jnp.minimum(zf, 0.0) - jnp.log1p(jnp.exp(-jnp.abs(zf)))
    pos = lax.broadcasted_iota(jnp.int32, (gate_rows, tm), 1)
    d = 1
    while d < tm:
        c = c + jnp.where(pos >= d, pltpu.roll(c, d, axis=1), 0.0)
        d *= 2
    fcum = c + fcarry_ref[:, :1]
    fcarry_ref[...] = jnp.broadcast_to(fcum[:, tm - 1:tm], fcarry_ref.shape)

    f_hi, f_mid, f_lo = _split3(fcum * LOG2_E)
    pieces = jnp.concatenate(
        [f_hi, f_mid, f_lo, jnp.zeros((LANES - 3 * gate_rows, tm), F32)], axis=0).astype(BF16)

    zg = _dot(h, wg_ref[...])
    gb = zg[:, :conv_ch]
    p = zg[:, conv_ch:2 * conv_ch] * zg[:, 2 * conv_ch:]
    yc = gb * _causal_conv3(p, pcarry_ref[...], wconv_ref[...])
    pcarry_ref[...] = p[tm - SUBLANES:]
    msc = jnp.mean(yc * yc, axis=-1, keepdims=True)
    mixc_ref[0] = (yc * lax.rsqrt(msc + EPS) * gconv_ref[...]).astype(BF16)

    routed = _dot(sel_ref[...], pieces)

    scale = LOG2_E / (HEAD_DIM ** 0.5)
    extra_row = lax.broadcasted_iota(jnp.int32, (LANES - HEAD_DIM, tm), 0)
    for hd in range(n_heads):
        q_extra = jnp.where((extra_row >= 6 * hd) & (extra_row < 6 * hd + 3), routed,
                            jnp.where((extra_row >= 6 * hd + 3) & (extra_row < 6 * hd + 6), 1.0, 0.0))
        qt_ref[0, hd] = jnp.concatenate(
            [vf[attn_w + hd * HEAD_DIM:attn_w + (hd + 1) * HEAD_DIM] * scale, q_extra], axis=0).astype(BF16)

    extras = jnp.concatenate([jnp.zeros((HEAD_DIM, tm), F32), routed], axis=0).T
    lane = lax.broadcasted_iota(jnp.int32, (tm, LANES), 1)
    for grp in range(n_heads // 4):
        zk = _dot(h, wk_ref[:, 2 * grp * LANES:2 * (grp + 1) * LANES])
        for hd in range(4 * grp, 4 * grp + 4):
            kb = zk[:, ((hd // 2) % 2) * LANES:((hd // 2) % 2 + 1) * LANES]
            if hd % 2:
                kb = pltpu.roll(kb, HEAD_DIM, axis=1)
            m0 = m0_ref[hd:hd + 1, :]
            m1 = m1_ref[hd:hd + 1, :]
            k_ref[0, hd] = jnp.where(lane < HEAD_DIM, kb, extras * m1 + m0).astype(BF16)


def _inproj(x, g_mix, w_in, b_f, w_conv, g_conv_out, *, n_heads, conv_ch):
    b, s, d = x.shape
    attn_w = n_heads * HEAD_DIM
    tm = ROW_TILE_IN
    gate_rows = -(-n_heads // SUBLANES) * SUBLANES
    extra = LANES - HEAD_DIM
    assert s % tm == 0 and n_heads % 4 == 0 and 6 * n_heads <= extra and 3 * gate_rows <= LANES
    o3 = 3 * conv_ch
    wg = w_in[:, :o3].astype(BF16)
    wk = w_in[:, o3 + attn_w:o3 + 2 * attn_w].astype(BF16)
    wvt = jnp.concatenate(
        [w_in[:, o3 + 2 * attn_w:o3 + 3 * attn_w].T, w_in[:, o3:o3 + attn_w].T,
         jnp.pad(w_in[:, o3 + 3 * attn_w:].T, ((0, 2 * SUBLANES - n_heads), (0, 0)))], axis=0).astype(BF16)
    bft = jnp.broadcast_to(jnp.pad(b_f, (0, gate_rows - n_heads))[:, None], (gate_rows, tm))
    wconv = jnp.pad(w_conv, ((0, SUBLANES - CONV_K), (0, 0)))
    hh = np.arange(n_heads)
    sel = np.zeros((extra, LANES), np.float32)
    m0 = np.zeros((n_heads, LANES), np.float32)
    m1 = np.zeros((n_heads, LANES), np.float32)
    for j in range(3):
        sel[6 * hh + j, j * gate_rows + hh] = 1.0
        sel[6 * hh + 3 + j, j * gate_rows + hh] = -1.0
        m0[hh, HEAD_DIM + 6 * hh + j] = 1.0
        m1[hh, HEAD_DIM + 6 * hh + 3 + j] = 1.0

    const = lambda shape: pl.BlockSpec(shape, lambda bi, i: (0,) * len(shape),
                                       pipeline_mode=pl.Buffered(1))
    head_rows = pl.BlockSpec((1, n_heads, tm, LANES), lambda bi, i: (bi, 0, i, 0))
    return pl.pallas_call(
        functools.partial(_inproj_kernel, n_heads=n_heads, conv_ch=conv_ch),
        grid=(b, s // tm),
        in_specs=[
            pl.BlockSpec((1, tm, d), lambda bi, i: (bi, i, 0)),
            const((1, d)), const(wg.shape), const(wk.shape), const(wvt.shape), const(bft.shape),
            const(sel.shape), const(m0.shape), const(m1.shape), const(wconv.shape), const((1, conv_ch)),
        ],
        out_specs=[
            pl.BlockSpec((1, tm, conv_ch), lambda bi, i: (bi, i, 0)),
            pl.BlockSpec((1, n_heads, LANES, tm), lambda bi, i: (bi, 0, 0, i)),
            head_rows,
            pl.BlockSpec((1, n_heads, V_ROWS, tm), lambda bi, i: (bi, 0, 0, i)),
        ],
        out_shape=[
            jax.ShapeDtypeStruct((b, s, conv_ch), BF16),
            jax.ShapeDtypeStruct((b, n_heads, LANES, s), BF16),
            jax.ShapeDtypeStruct((b, n_heads, s, LANES), BF16),
            jax.ShapeDtypeStruct((b, n_heads, V_ROWS, s), BF16),
        ],
        scratch_shapes=[pltpu.VMEM((SUBLANES, conv_ch), F32), pltpu.VMEM((gate_rows, LANES), F32)],
        compiler_params=pltpu.CompilerParams(
            dimension_semantics=("arbitrary", "arbitrary"),
            vmem_limit_bytes=_vmem_limit(
                ((tm, d), F32, 2), (wg.shape, BF16, 1), (wk.shape, BF16, 1), (wvt.shape, BF16, 1),
                ((tm, conv_ch), BF16, 2), ((n_heads, 2 * LANES + V_ROWS, tm), BF16, 2))),
        name="inproj",
    )(x, g_mix.reshape(1, d), wg, wk, wvt, bft, jnp.asarray(sel, BF16), jnp.asarray(m0), jnp.asarray(m1),
      wconv, g_conv_out.reshape(1, conv_ch))


def _attn_kernel(qt_ref, k_ref, vt_ref, o_ref, s_ref, acc_ref):
    nh = k_ref.shape[1]
    s = k_ref.shape[2]
    tq, tk = ATTN_Q_TILE, ATTN_K_TILE
    r = tq // tk

    def causal(d):
        shape = (tk, tq - d * tk)
        return lax.broadcasted_iota(jnp.int32, shape, 0) <= lax.broadcasted_iota(jnp.int32, shape, 1)

    def load_q(qi):
        q0 = pl.multiple_of(qi * tq, tq)
        return [qt_ref[0, hh, :, pl.ds(q0, tq)] for hh in range(nh)]

    def issue_scores(j, slot, qs):
        k0 = pl.multiple_of(j * tk, tk)
        maxima = []
        for hh in range(nh):
            st = _dot(k_ref[0, hh, pl.ds(k0, tk), :], qs[hh])
            s_ref[hh, slot] = st
            maxima.append(jnp.max(st, axis=0, keepdims=True))
        return tuple(maxima)

    def q_tile(qi, first_max):
        q0 = pl.multiple_of(qi * tq, tq)
        qs = load_q(qi)

        def consume(j, slot, ms, tile_max):
            k0 = pl.multiple_of(j * tk, tk)
            out = []
            for hh in range(nh):
                m_new = jnp.maximum(ms[hh], tile_max[hh])
                p = jnp.exp2(s_ref[hh, slot] - m_new).astype(BF16)
                alpha = jnp.exp2(ms[hh] - m_new)
                acc_ref[hh] = acc_ref[hh] * alpha + _dot(vt_ref[0, hh, :, pl.ds(k0, tk)], p)
                out.append(m_new)
            return tuple(out)

        def kv_pair(jj, state):
            ms, max0 = state
            max1 = issue_scores(2 * jj + 1, 1, qs)
            ms = consume(2 * jj, 0, ms, max0)
            max0 = issue_scores(2 * jj + 2, 0, qs)
            return consume(2 * jj + 1, 1, ms, max1), max0

        acc_ref[...] = jnp.zeros_like(acc_ref)
        state = (tuple(jnp.full((1, tq), NEG_BIG, F32) for _ in range(nh)), first_max)
        def pairs(j0, n, st):
            for u in range(n):
                st = kv_pair(j0 + u, st)
            return st

        n_pairs = qi * (r // 2)
        n1 = n_pairs % 2
        n2 = 2 * ((n_pairs // 2) % 2)
        state = lax.cond(n1 == 1, lambda st: pairs(0, 1, st), lambda st: st, state)
        state = lax.cond(n2 == 2, lambda st: pairs(n1, 2, st), lambda st: st, state)
        ms, _ = lax.fori_loop(0, n_pairs // 4, lambda i, st: pairs(n1 + n2 + 4 * i, 4, st), state)

        def diag_cols(d):
            return slice(d * tk, tq), pl.multiple_of((r * qi + d) * tk, tk)

        def issue_diag(d):
            cols, k0 = diag_cols(d)
            for hh in range(nh):
                s_ref[hh, d % 2, :, cols] = _dot(k_ref[0, hh, pl.ds(k0, tk), :], qs[hh][:, d * tk:])

        def consume_diag(d, ms):
            cols, k0 = diag_cols(d)
            out = []
            for hh in range(nh):
                st = jnp.where(causal(d), s_ref[hh, d % 2, :, cols], NEG_BIG)
                m_old = ms[hh][:, cols]
                m_new = jnp.maximum(m_old, jnp.max(st, axis=0, keepdims=True))
                p = jnp.exp2(st - m_new).astype(BF16)
                acc_ref[hh, :, cols] = (acc_ref[hh, :, cols] * jnp.exp2(m_old - m_new)
                                        + _dot(vt_ref[0, hh, :, pl.ds(k0, tk)], p))
                out.append(m_new if d == 0 else jnp.concatenate([ms[hh][:, :d * tk], m_new], axis=1))
            return tuple(out)

        issue_diag(1)
        for d in range(r):
            ms = consume_diag(d, ms)
            if d + 2 < r:
                issue_diag(d + 2)
            elif d == r - 2:
                next_max = issue_scores(0, 0, load_q(jnp.minimum(qi + 1, s // tq - 1)))
        for hh in range(nh):
            acc = acc_ref[hh]
            o = acc[:HEAD_DIM] / acc[HEAD_DIM:HEAD_DIM + 1]
            o_ref[0, hh * HEAD_DIM:(hh + 1) * HEAD_DIM, pl.ds(q0, tq)] = o.astype(o_ref.dtype)
        return next_max

    lax.fori_loop(0, s // tq, q_tile, issue_scores(0, 0, load_q(0)))


def _attention(qt_aug, k_aug, vt_aug):
    b, n_heads, s, _ = k_aug.shape
    nh = ATTN_HEADS_PER_STEP
    tq, tk = ATTN_Q_TILE, ATTN_K_TILE
    assert s % tq == 0 and tq % (2 * tk) == 0 and n_heads % nh == 0
    rows = pl.BlockSpec((1, nh, s, LANES), lambda bi, hi: (bi, hi, 0, 0))
    return pl.pallas_call(
        _attn_kernel,
        grid=(b, n_heads // nh),
        in_specs=[pl.BlockSpec((1, nh, LANES, s), lambda bi, hi: (bi, hi, 0, 0)), rows,
                  pl.BlockSpec((1, nh, V_ROWS, s), lambda bi, hi: (bi, hi, 0, 0))],
        out_specs=pl.BlockSpec((1, nh * HEAD_DIM, s), lambda bi, hi: (bi, hi, 0)),
        out_shape=jax.ShapeDtypeStruct((b, n_heads * HEAD_DIM, s), BF16),
        scratch_shapes=[pltpu.VMEM((nh, ATTN_SLOTS, tk, tq), F32), pltpu.VMEM((nh, V_ROWS, tq), F32)],
        compiler_params=pltpu.CompilerParams(
            dimension_semantics=("arbitrary", "arbitrary"),
            vmem_limit_bytes=_vmem_limit(
                ((nh, 2 * LANES + V_ROWS + HEAD_DIM, s), BF16, 2),
                ((nh, ATTN_SLOTS, tk, tq), F32, 1), ((nh, V_ROWS, tq), F32, 1))),
        name="fox_attention",
    )(qt_aug, k_aug, vt_aug)


def _post_kernel(x_ref, mixc_ref, yt_ref, gattn_ref, wo_ref, gffn_ref, wup_ref, wfc_ref, wdown_ref,
                 gfin_ref, out_ref, ucarry_ref, h2_ref, up_ref, act_ref, *, d_ff, final_norm):
    tm = x_ref.shape[1]

    @pl.when(pl.program_id(1) == 0)
    def _():
        ucarry_ref[...] = jnp.zeros_like(ucarry_ref)

    yt = yt_ref[0].astype(F32)
    msa = jnp.mean(yt * yt, axis=0, keepdims=True)
    yn = (yt * lax.rsqrt(msa + EPS) * gattn_ref[...]).T.astype(BF16)
    mix = jnp.concatenate([mixc_ref[0], yn], axis=1)
    x1 = x_ref[0] + _dot(mix, wo_ref[...])

    ms = jnp.mean(x1 * x1, axis=-1, keepdims=True)
    h2 = (x1 * lax.rsqrt(ms + EPS) * gffn_ref[...]).astype(BF16)

    out_ref[0] = x1
    h2_ref[...] = h2
    starts = list(range(0, d_ff, FF_CHUNK))
    n_chunks = len(starts)
    width = lambda c: min(FF_CHUNK, d_ff - starts[c])
    cols = lambda c: (slice(starts[c], starts[c] + width(c)),
                      slice(d_ff + starts[c], d_ff + starts[c] + width(c)))

    def issue_up(c):
        w = width(c)
        for half, cc in enumerate(cols(c)):
            up_ref[c % FF_UP_SLOTS, half, :SUBLANES, :w] = ucarry_ref[:, cc]
            up_ref[c % FF_UP_SLOTS, half, SUBLANES:, :w] = _dot(h2_ref[...], wup_ref[:, cc])

    def gate_and_down(c):
        ca, cg = cols(c)
        w = width(c)
        slot = c % FF_UP_SLOTS
        for r0 in range(0, tm, FF_ROW_BLOCK):
            ext = slice(r0, r0 + FF_ROW_BLOCK + SUBLANES)
            a = _causal_conv3_ext(up_ref[slot, 0, ext, :w], wfc_ref[:, ca])
            g = _causal_conv3_ext(up_ref[slot, 1, ext, :w], wfc_ref[:, cg])
            act_ref[c % 2, r0:r0 + FF_ROW_BLOCK, :w] = (g * jax.nn.sigmoid(g) * a).astype(BF16)
        ucarry_ref[:, ca] = up_ref[slot, 0, tm:, :w]
        ucarry_ref[:, cg] = up_ref[slot, 1, tm:, :w]
        out_ref[0] += _dot(act_ref[c % 2, :, :w], wdown_ref[ca, :])

    for c in range(min(FF_UP_SLOTS - 1, n_chunks)):
        issue_up(c)
    for c in range(n_chunks):
        if c + FF_UP_SLOTS - 1 < n_chunks:
            issue_up(c + FF_UP_SLOTS - 1)
        gate_and_down(c)

    if final_norm:
        acc = out_ref[0]
        msf = jnp.mean(acc * acc, axis=-1, keepdims=True)
        out_ref[0] = acc * lax.rsqrt(msf + EPS) * gfin_ref[...]


def _post(x, mixc, yt, g_attn_out, w_o, g_ffn, w_up, w_ffn_conv, w_down, g_final, *, final_norm):
    b, s, d = x.shape
    conv_ch = mixc.shape[-1]
    attn_w = yt.shape[1]
    d_ff = w_down.shape[0]
    tm = ROW_TILE_POST
    assert s % tm == 0 and d_ff % LANES == 0 and FF_CHUNK % LANES == 0
    wfc = jnp.pad(w_ffn_conv, ((0, SUBLANES - CONV_K), (0, 0)))

    const = lambda shape: pl.BlockSpec(shape, lambda bi, i: (0,) * len(shape),
                                       pipeline_mode=pl.Buffered(1))
    return pl.pallas_call(
        functools.partial(_post_kernel, d_ff=d_ff, final_norm=final_norm),
        grid=(b, s // tm),
        in_specs=[
            pl.BlockSpec((1, tm, d), lambda bi, i: (bi, i, 0)),
            pl.BlockSpec((1, tm, conv_ch), lambda bi, i: (bi, i, 0)),
            pl.BlockSpec((1, attn_w, tm), lambda bi, i: (bi, 0, i)),
            const((attn_w, 1)), const(w_o.shape), const((1, d)), const(w_up.shape),
            const(wfc.shape), const(w_down.shape), const((1, d)),
        ],
        out_specs=pl.BlockSpec((1, tm, d), lambda bi, i: (bi, i, 0)),
        out_shape=jax.ShapeDtypeStruct((b, s, d), F32),
        scratch_shapes=[pltpu.VMEM((SUBLANES, 2 * d_ff), F32), pltpu.VMEM((tm, d), BF16),
                        pltpu.VMEM((FF_UP_SLOTS, 2, tm + SUBLANES, FF_CHUNK), F32),
                        pltpu.VMEM((2, tm, FF_CHUNK), BF16)],
        compiler_params=pltpu.CompilerParams(
            dimension_semantics=("arbitrary", "arbitrary"),
            vmem_limit_bytes=_vmem_limit(
                ((tm, d), F32, 4), ((tm, conv_ch + attn_w), BF16, 2),
                (w_o.shape, BF16, 1), (w_up.shape, BF16, 1), (w_down.shape, BF16, 1),
                ((tm, d), BF16, 1), ((FF_UP_SLOTS, 2, tm + SUBLANES, FF_CHUNK), F32, 1),
                ((2, tm, FF_CHUNK), BF16, 1))),
        name="post_ffn",
    )(x, mixc, yt, g_attn_out.reshape(attn_w, 1), w_o.astype(BF16), g_ffn.reshape(1, d),
      w_up.astype(BF16), wfc, w_down.astype(BF16), g_final.reshape(1, d))


def kernel(x, g_mix, w_in, b_f, w_conv, g_conv_out, g_attn_out, w_o, g_ffn, w_up, w_ffn_conv, w_down,
           g_final):
    depth = w_in.shape[0]
    conv_ch = w_conv.shape[-1]
    n_heads = b_f.shape[-1]
    for l in range(depth):
        mixc, qt_aug, k_aug, vt_aug = _inproj(x, g_mix[l], w_in[l], b_f[l], w_conv[l], g_conv_out[l],
                                             n_heads=n_heads, conv_ch=conv_ch)
        yt = _attention(qt_aug, k_aug, vt_aug)
        x = _post(x, mixc, yt, g_attn_out[l], w_o[l], g_ffn[l], w_up[l], w_ffn_conv[l], w_down[l],
                  g_final, final_norm=(l == depth - 1))
    return x
```

```python
import functools

import jax
import jax.numpy as jnp
import numpy as np
from jax import lax
from jax.experimental import pallas as pl
from jax.experimental.pallas import tpu as pltpu

EPS = 1e-6
CONV_K = 3
HEAD_DIM = 64
LANES = 128
SUBLANES = 8
NEG_BIG = -1e30
LOG2_E = 1.4426950408889634
V_ROWS = HEAD_DIM + 16

ROW_TILE_IN = 1024
ROW_TILE_POST = 512
ATTN_Q_TILE = 512
ATTN_K_TILE = 256
ATTN_HEADS_PER_STEP = 2
ATTN_SLOTS = 2
ATTN_PAIRS_PER_TRIP = 8
FF_CHUNK = 1024
FF_UP_SLOTS = 3
FF_ROW_BLOCK = 64

BF16 = jnp.bfloat16
F32 = jnp.float32

V7X_VMEM_BYTES = 64 * 1024 * 1024
VMEM_REQUEST_CAP = V7X_VMEM_BYTES * 7 // 8
VMEM_TEMPORARIES = 1.5


def _vmem_limit(*buffers):
    need = sum(count * int(np.prod(shape)) * jnp.dtype(dtype).itemsize for shape, dtype, count in buffers)
    return min(VMEM_REQUEST_CAP, int(need * VMEM_TEMPORARIES))


def _dot(a, b):
    return jnp.dot(a, b, preferred_element_type=F32)


def _dot_nt(a, b):
    return lax.dot_general(a, b, (((1,), (1,)), ((), ())), preferred_element_type=F32)


def _split3(v):
    hi = v.astype(BF16).astype(F32)
    r1 = v - hi
    mid = r1.astype(BF16).astype(F32)
    lo = r1 - mid
    return hi, mid, lo


def _shift_rows(cur, prev8, k):
    ext = jnp.concatenate([prev8, cur], axis=0)
    return pltpu.roll(ext, k, axis=0)[SUBLANES:]


def _causal_conv3(cur, prev8, w):
    return (_shift_rows(cur, prev8, 2) * w[0:1] + _shift_rows(cur, prev8, 1) * w[1:2]
            + cur * w[2:3])


def _causal_conv3_ext(ext, w):
    return (pltpu.roll(ext, 2, axis=0)[SUBLANES:] * w[0:1] + pltpu.roll(ext, 1, axis=0)[SUBLANES:] * w[1:2]
            + ext[SUBLANES:] * w[2:3])


def _inproj_kernel(x_ref, gmix_ref, wg_ref, wk_ref, wvt_ref, bft_ref, sel_ref, m0_ref, m1_ref,
                   wconv_ref, gconv_ref, mixc_ref, qt_ref, k_ref, vt_ref, pcarry_ref, fcarry_ref,
                   *, n_heads, conv_ch):
    tm = x_ref.shape[1]
    attn_w = n_heads * HEAD_DIM
    gate_rows = fcarry_ref.shape[0]

    @pl.when(pl.program_id(1) == 0)
    def _():
        pcarry_ref[...] = jnp.zeros_like(pcarry_ref)
        fcarry_ref[...] = jnp.zeros_like(fcarry_ref)

    x = x_ref[0]
    ms = jnp.mean(x * x, axis=-1, keepdims=True)
    h = (x * lax.rsqrt(ms + EPS) * gmix_ref[...]).astype(BF16)

    vf = _dot_nt(wvt_ref[...], h)
    ones_rows = jnp.where(lax.broadcasted_iota(jnp.int32, (V_ROWS - HEAD_DIM, tm), 0) == 0, 1.0, 0.0)
    for hd in range(n_heads):
        vt_ref[0, hd] = jnp.concatenate(
            [vf[hd * HEAD_DIM:(hd + 1) * HEAD_DIM], ones_rows], axis=0).astype(BF16)

    zf = vf[2 * attn_w:2 * attn_w + gate_rows] + bft_ref[...]
    c = jnp.minimum(zf, 0.0) - jnp.log1p(jnp.exp(-jnp.abs(zf)))
    pos = lax.broadcasted_iota(jnp.int32, (gate_rows, tm), 1)
    d = 1
    while d < tm:
        c = c + jnp.where(pos >= d, pltpu.roll(c, d, axis=1), 0.0)
        d *= 2
    fcum = c + fcarry_ref[:, :1]
    fcarry_ref[...] = jnp.broadcast_to(fcum[:, tm - 1:tm], fcarry_ref.shape)

    f_hi, f_mid, f_lo = _split3(fcum * LOG2_E)
    pieces = jnp.concatenate(
        [f_hi, f_mid, f_lo, jnp.zeros((LANES - 3 * gate_rows, tm), F32)], axis=0).astype(BF16)

    zg = _dot(h, wg_ref[...])
    gb = zg[:, :conv_ch]
    p = zg[:, conv_ch:2 * conv_ch] * zg[:, 2 * conv_ch:]
    yc = gb * _causal_conv3(p, pcarry_ref[...], wconv_ref[...])
    pcarry_ref[...] = p[tm - SUBLANES:]
    msc = jnp.mean(yc * yc, axis=-1, keepdims=True)
    mixc_ref[0] = (yc * lax.rsqrt(msc + EPS) * gconv_ref[...]).astype(BF16)

    routed = _dot(sel_ref[...], pieces)

    scale = LOG2_E / (HEAD_DIM ** 0.5)
    extra_row = lax.broadcasted_iota(jnp.int32, (LANES - HEAD_DIM, tm), 0)
    for hd in range(n_heads):
        q_extra = jnp.where((extra_row >= 6 * hd) & (extra_row < 6 * hd + 3), routed,
                            jnp.where((extra_row >= 6 * hd + 3) & (extra_row < 6 * hd + 6), 1.0, 0.0))
        qt_ref[0, hd] = jnp.concatenate(
            [vf[attn_w + hd * HEAD_DIM:attn_w + (hd + 1) * HEAD_DIM] * scale, q_extra], axis=0).astype(BF16)

    extras = jnp.concatenate([jnp.zeros((HEAD_DIM, tm), F32), routed], axis=0).T
    lane = lax.broadcasted_iota(jnp.int32, (tm, LANES), 1)
    for grp in range(n_heads // 4):
        zk = _dot(h, wk_ref[:, 2 * grp * LANES:2 * (grp + 1) * LANES])
        for hd in range(4 * grp, 4 * grp + 4):
            kb = zk[:, ((hd // 2) % 2) * LANES:((hd // 2) % 2 + 1) * LANES]
            if hd % 2:
                kb = pltpu.roll(kb, HEAD_DIM, axis=1)
            m0 = m0_ref[hd:hd + 1, :]
            m1 = m1_ref[hd:hd + 1, :]
            k_ref[0, hd] = jnp.where(lane < HEAD_DIM, kb, extras * m1 + m0).astype(BF16)


def _inproj(x, g_mix, w_in, b_f, w_conv, g_conv_out, *, n_heads, conv_ch):
    b, s, d = x.shape
    attn_w = n_heads * HEAD_DIM
    tm = ROW_TILE_IN
    gate_rows = -(-n_heads // SUBLANES) * SUBLANES
    extra = LANES - HEAD_DIM
    assert s % tm == 0 and n_heads % 4 == 0 and 6 * n_heads <= extra and 3 * gate_rows <= LANES
    o3 = 3 * conv_ch
    wg = w_in[:, :o3].astype(BF16)
    wk = w_in[:, o3 + attn_w:o3 + 2 * attn_w].astype(BF16)
    wvt = jnp.concatenate(
        [w_in[:, o3 + 2 * attn_w:o3 + 3 * attn_w].T, w_in[:, o3:o3 + attn_w].T,
         jnp.pad(w_in[:, o3 + 3 * attn_w:].T, ((0, 2 * SUBLANES - n_heads), (0, 0)))], axis=0).astype(BF16)
    bft = jnp.broadcast_to(jnp.pad(b_f, (0, gate_rows - n_heads))[:, None], (gate_rows, tm))
    wconv = jnp.pad(w_conv, ((0, SUBLANES - CONV_K), (0, 0)))
    hh = np.arange(n_heads)
    sel = np.zeros((extra, LANES), np.float32)
    m0 = np.zeros((n_heads, LANES), np.float32)
    m1 = np.zeros((n_heads, LANES), np.float32)
    for j in range(3):
        sel[6 * hh + j, j * gate_rows + hh] = 1.0
        sel[6 * hh + 3 + j, j * gate_rows + hh] = -1.0
        m0[hh, HEAD_DIM + 6 * hh + j] = 1.0
        m1[hh, HEAD_DIM + 6 * hh + 3 + j] = 1.0

    const = lambda shape: pl.BlockSpec(shape, lambda bi, i: (0,) * len(shape),
                                       pipeline_mode=pl.Buffered(1))
    head_rows = pl.BlockSpec((1, n_heads, tm, LANES), lambda bi, i: (bi, 0, i, 0))
    return pl.pallas_call(
        functools.partial(_inproj_kernel, n_heads=n_heads, conv_ch=conv_ch),
        grid=(b, s // tm),
        in_specs=[
            pl.BlockSpec((1, tm, d), lambda bi, i: (bi, i, 0)),
            const((1, d)), const(wg.shape), const(wk.shape), const(wvt.shape), const(bft.shape),
            const(sel.shape), const(m0.shape), const(m1.shape), const(wconv.shape), const((1, conv_ch)),
        ],
        out_specs=[
            pl.BlockSpec((1, tm, conv_ch), lambda bi, i: (bi, i, 0)),
            pl.BlockSpec((1, n_heads, LANES, tm), lambda bi, i: (bi, 0, 0, i)),
            head_rows,
            pl.BlockSpec((1, n_heads, V_ROWS, tm), lambda bi, i: (bi, 0, 0, i)),
        ],
        out_shape=[
            jax.ShapeDtypeStruct((b, s, conv_ch), BF16),
            jax.ShapeDtypeStruct((b, n_heads, LANES, s), BF16),
            jax.ShapeDtypeStruct((b, n_heads, s, LANES), BF16),
            jax.ShapeDtypeStruct((b, n_heads, V_ROWS, s), BF16),
        ],
        scratch_shapes=[pltpu.VMEM((SUBLANES, conv_ch), F32), pltpu.VMEM((gate_rows, LANES), F32)],
        compiler_params=pltpu.CompilerParams(
            dimension_semantics=("arbitrary", "arbitrary"),
            vmem_limit_bytes=_vmem_limit(
                ((tm, d), F32, 2), (wg.shape, BF16, 1), (wk.shape, BF16, 1), (wvt.shape, BF16, 1),
                ((tm, conv_ch), BF16, 2), ((n_heads, 2 * LANES + V_ROWS, tm), BF16, 2))),
        name="inproj",
    )(x, g_mix.reshape(1, d), wg, wk, wvt, bft, jnp.asarray(sel, BF16), jnp.asarray(m0), jnp.asarray(m1),
      wconv, g_conv_out.reshape(1, conv_ch))


def _attn_kernel(qt_ref, k_ref, vt_ref, o_ref, s_ref, acc_ref):
    nh = k_ref.shape[1]
    s = k_ref.shape[2]
    tq, tk = ATTN_Q_TILE, ATTN_K_TILE
    r = tq // tk

    def causal(d):
        shape = (tk, tq - d * tk)
        return lax.broadcasted_iota(jnp.int32, shape, 0) <= lax.broadcasted_iota(jnp.int32, shape, 1)

    def load_q(qi):
        q0 = pl.multiple_of(qi * tq, tq)
        return [qt_ref[0, hh, :, pl.ds(q0, tq)] for hh in range(nh)]

    def issue_scores(j, slot, qs):
        k0 = pl.multiple_of(j * tk, tk)
        maxima = []
        for hh in range(nh):
            st = _dot(k_ref[0, hh, pl.ds(k0, tk), :], qs[hh])
            s_ref[hh, slot] = st
            maxima.append(jnp.max(st, axis=0, keepdims=True))
        return tuple(maxima)

    def q_tile(qi, first_max):
        q0 = pl.multiple_of(qi * tq, tq)
        qs = load_q(qi)

        def consume(j, slot, ms, tile_max):
            k0 = pl.multiple_of(j * tk, tk)
            out = []
            for hh in range(nh):
                m_new = jnp.maximum(ms[hh], tile_max[hh])
                p = jnp.exp2(s_ref[hh, slot] - m_new).astype(BF16)
                alpha = jnp.exp2(ms[hh] - m_new)
                acc_ref[hh] = acc_ref[hh] * alpha + _dot(vt_ref[0, hh, :, pl.ds(k0, tk)], p)
                out.append(m_new)
            return tuple(out)

        def kv_pair(jj, state):
            ms, max0 = state
            max1 = issue_scores(2 * jj + 1, 1, qs)
            ms = consume(2 * jj, 0, ms, max0)
            max0 = issue_scores(2 * jj + 2, 0, qs)
            return consume(2 * jj + 1, 1, ms, max1), max0

        acc_ref[...] = jnp.zeros_like(acc_ref)
        state = (tuple(jnp.full((1, tq), NEG_BIG, F32) for _ in range(nh)), first_max)
        def pairs(j0, n, st):
            for u in range(n):
                st = kv_pair(j0 + u, st)
            return st

        n_pairs = qi * (r // 2)
        done = 0
        group = 1
        while group < ATTN_PAIRS_PER_TRIP:
            take = group * ((n_pairs // group) % 2)
            state = lax.cond(take > 0, functools.partial(pairs, done, group), lambda st: st, state)
            done = done + take
            group *= 2
        ms, _ = lax.fori_loop(
            0, n_pairs // ATTN_PAIRS_PER_TRIP,
            lambda i, st: pairs(done + ATTN_PAIRS_PER_TRIP * i, ATTN_PAIRS_PER_TRIP, st), state)

        def diag_cols(d):
            return slice(d * tk, tq), pl.multiple_of((r * qi + d) * tk, tk)

        def issue_diag(d):
            cols, k0 = diag_cols(d)
            for hh in range(nh):
                s_ref[hh, d % 2, :, cols] = _dot(k_ref[0, hh, pl.ds(k0, tk), :], qs[hh][:, d * tk:])

        def consume_diag(d, ms):
            cols, k0 = diag_cols(d)
            out = []
            for hh in range(nh):
                st = jnp.where(causal(d), s_ref[hh, d % 2, :, cols], NEG_BIG)
                m_old = ms[hh][:, cols]
                m_new = jnp.maximum(m_old, jnp.max(st, axis=0, keepdims=True))
                p = jnp.exp2(st - m_new).astype(BF16)
                acc_ref[hh, :, cols] = (acc_ref[hh, :, cols] * jnp.exp2(m_old - m_new)
                                        + _dot(vt_ref[0, hh, :, pl.ds(k0, tk)], p))
                out.append(m_new if d == 0 else jnp.concatenate([ms[hh][:, :d * tk], m_new], axis=1))
            return tuple(out)

        issue_diag(1)
        for d in range(r):
            ms = consume_diag(d, ms)
            if d + 2 < r:
                issue_diag(d + 2)
            elif d == r - 2:
                next_max = issue_scores(0, 0, load_q(jnp.minimum(qi + 1, s // tq - 1)))
        for hh in range(nh):
            acc = acc_ref[hh]
            o = acc[:HEAD_DIM] / acc[HEAD_DIM:HEAD_DIM + 1]
            o_ref[0, hh * HEAD_DIM:(hh + 1) * HEAD_DIM, pl.ds(q0, tq)] = o.astype(o_ref.dtype)
        return next_max

    lax.fori_loop(0, s // tq, q_tile, issue_scores(0, 0, load_q(0)))


def _attention(qt_aug, k_aug, vt_aug):
    b, n_heads, s, _ = k_aug.shape
    nh = ATTN_HEADS_PER_STEP
    tq, tk = ATTN_Q_TILE, ATTN_K_TILE
    assert s % tq == 0 and tq % (2 * tk) == 0 and n_heads % nh == 0
    rows = pl.BlockSpec((1, nh, s, LANES), lambda bi, hi: (bi, hi, 0, 0))
    return pl.pallas_call(
        _attn_kernel,
        grid=(b, n_heads // nh),
        in_specs=[pl.BlockSpec((1, nh, LANES, s), lambda bi, hi: (bi, hi, 0, 0)), rows,
                  pl.BlockSpec((1, nh, V_ROWS, s), lambda bi, hi: (bi, hi, 0, 0))],
        out_specs=pl.BlockSpec((1, nh * HEAD_DIM, s), lambda bi, hi: (bi, hi, 0)),
        out_shape=jax.ShapeDtypeStruct((b, n_heads * HEAD_DIM, s), BF16),
        scratch_shapes=[pltpu.VMEM((nh, ATTN_SLOTS, tk, tq), F32), pltpu.VMEM((nh, V_ROWS, tq), F32)],
        compiler_params=pltpu.CompilerParams(
            dimension_semantics=("arbitrary", "arbitrary"),
            vmem_limit_bytes=_vmem_limit(
                ((nh, 2 * LANES + V_ROWS + HEAD_DIM, s), BF16, 2),
                ((nh, ATTN_SLOTS, tk, tq), F32, 1), ((nh, V_ROWS, tq), F32, 1))),
        name="fox_attention",
    )(qt_aug, k_aug, vt_aug)


def _post_kernel(x_ref, mixc_ref, yt_ref, gattn_ref, wo_ref, gffn_ref, wup_ref, wfc_ref, wdown_ref,
                 gfin_ref, out_ref, ucarry_ref, h2_ref, up_ref, act_ref, *, d_ff, final_norm):
    tm = x_ref.shape[1]

    @pl.when(pl.program_id(1) == 0)
    def _():
        ucarry_ref[...] = jnp.zeros_like(ucarry_ref)

    yt = yt_ref[0].astype(F32)
    msa = jnp.mean(yt * yt, axis=0, keepdims=True)
    yn = (yt * lax.rsqrt(msa + EPS) * gattn_ref[...]).T.astype(BF16)
    mix = jnp.concatenate([mixc_ref[0], yn], axis=1)
    x1 = x_ref[0] + _dot(mix, wo_ref[...])

    ms = jnp.mean(x1 * x1, axis=-1, keepdims=True)
    h2 = (x1 * lax.rsqrt(ms + EPS) * gffn_ref[...]).astype(BF16)

    out_ref[0] = x1
    h2_ref[...] = h2
    starts = list(range(0, d_ff, FF_CHUNK))
    n_chunks = len(starts)
    width = lambda c: min(FF_CHUNK, d_ff - starts[c])
    cols = lambda c: (slice(starts[c], starts[c] + width(c)),
                      slice(d_ff + starts[c], d_ff + starts[c] + width(c)))

    def issue_up(c):
        w = width(c)
        for half, cc in enumerate(cols(c)):
            up_ref[c % FF_UP_SLOTS, half, :SUBLANES, :w] = ucarry_ref[:, cc]
            up_ref[c % FF_UP_SLOTS, half, SUBLANES:, :w] = _dot(h2_ref[...], wup_ref[:, cc])

    def gate_and_down(c):
        ca, cg = cols(c)
        w = width(c)
        slot = c % FF_UP_SLOTS
        for r0 in range(0, tm, FF_ROW_BLOCK):
            ext = slice(r0, r0 + FF_ROW_BLOCK + SUBLANES)
            a = _causal_conv3_ext(up_ref[slot, 0, ext, :w], wfc_ref[:, ca])
            g = _causal_conv3_ext(up_ref[slot, 1, ext, :w], wfc_ref[:, cg])
            act_ref[c % 2, r0:r0 + FF_ROW_BLOCK, :w] = (g * jax.nn.sigmoid(g) * a).astype(BF16)
        ucarry_ref[:, ca] = up_ref[slot, 0, tm:, :w]
        ucarry_ref[:, cg] = up_ref[slot, 1, tm:, :w]
        acc = out_ref[0] + _dot(act_ref[c % 2, :, :w], wdown_ref[ca, :])
        if final_norm and c == n_chunks - 1:
            msf = jnp.mean(acc * acc, axis=-1, keepdims=True)
            acc = acc * lax.rsqrt(msf + EPS) * gfin_ref[...]
        out_ref[0] = acc

    for c in range(min(FF_UP_SLOTS - 1, n_chunks)):
        issue_up(c)
    for c in range(n_chunks):
        if c + FF_UP_SLOTS - 1 < n_chunks:
            issue_up(c + FF_UP_SLOTS - 1)
        gate_and_down(c)


def _post(x, mixc, yt, g_attn_out, w_o, g_ffn, w_up, w_ffn_conv, w_down, g_final, *, final_norm):
    b, s, d = x.shape
    conv_ch = mixc.shape[-1]
    attn_w = yt.shape[1]
    d_ff = w_down.shape[0]
    tm = ROW_TILE_POST
    assert s % tm == 0 and d_ff % LANES == 0 and FF_CHUNK % LANES == 0
    wfc = jnp.pad(w_ffn_conv, ((0, SUBLANES - CONV_K), (0, 0)))

    const = lambda shape: pl.BlockSpec(shape, lambda bi, i: (0,) * len(shape),
                                       pipeline_mode=pl.Buffered(1))
    return pl.pallas_call(
        functools.partial(_post_kernel, d_ff=d_ff, final_norm=final_norm),
        grid=(b, s // tm),
        in_specs=[
            pl.BlockSpec((1, tm, d), lambda bi, i: (bi, i, 0)),
            pl.BlockSpec((1, tm, conv_ch), lambda bi, i: (bi, i, 0)),
            pl.BlockSpec((1, attn_w, tm), lambda bi, i: (bi, 0, i)),
            const((attn_w, 1)), const(w_o.shape), const((1, d)), const(w_up.shape),
            const(wfc.shape), const(w_down.shape), const((1, d)),
        ],
        out_specs=pl.BlockSpec((1, tm, d), lambda bi, i: (bi, i, 0)),
        out_shape=jax.ShapeDtypeStruct((b, s, d), F32),
        scratch_shapes=[pltpu.VMEM((SUBLANES, 2 * d_ff), F32), pltpu.VMEM((tm, d), BF16),
                        pltpu.VMEM((FF_UP_SLOTS, 2, tm + SUBLANES, FF_CHUNK), F32),
                        pltpu.VMEM((2, tm, FF_CHUNK), BF16)],
        compiler_params=pltpu.CompilerParams(
            dimension_semantics=("arbitrary", "arbitrary"),
            vmem_limit_bytes=_vmem_limit(
                ((tm, d), F32, 4), ((tm, conv_ch + attn_w), BF16, 2),
                (w_o.shape, BF16, 1), (w_up.shape, BF16, 1), (w_down.shape, BF16, 1),
                ((tm, d), BF16, 1), ((FF_UP_SLOTS, 2, tm + SUBLANES, FF_CHUNK), F32, 1),
                ((2, tm, FF_CHUNK), BF16, 1))),
        name="post_ffn",
    )(x, mixc, yt, g_attn_out.reshape(attn_w, 1), w_o.astype(BF16), g_ffn.reshape(1, d),
      w_up.astype(BF16), wfc, w_down.astype(BF16), g_final.reshape(1, d))


def kernel(x, g_mix, w_in, b_f, w_conv, g_conv_out, g_attn_out, w_o, g_ffn, w_up, w_ffn_conv, w_down,
           g_final):
    depth = w_in.shape[0]
    conv_ch = w_conv.shape[-1]
    n_heads = b_f.shape[-1]
    for l in range(depth):
        mixc, qt_aug, k_aug, vt_aug = _inproj(x, g_mix[l], w_in[l], b_f[l], w_conv[l], g_conv_out[l],
                                             n_heads=n_heads, conv_ch=conv_ch)
        yt = _attention(qt_aug, k_aug, vt_aug)
        x = _post(x, mixc, yt, g_attn_out[l], w_o[l], g_ffn[l], w_up[l], w_ffn_conv[l], w_down[l],
                  g_final, final_norm=(l == depth - 1))
    return x
```

```python
import functools

import jax
import jax.numpy as jnp
import numpy as np
from jax import lax
from jax.experimental import pallas as pl
from jax.experimental.pallas import tpu as pltpu

EPS = 1e-6
CONV_K = 3
HEAD_DIM = 64
LANES = 128
SUBLANES = 8
NEG_BIG = -1e30
LOG2_E = 1.4426950408889634
V_ROWS = HEAD_DIM + 16

ROW_TILE_IN = 1024
ROW_TILE_POST = 512
ATTN_Q_TILE = 512
ATTN_K_TILE = 256
ATTN_HEADS_PER_STEP = 2
ATTN_SLOTS = 2
ATTN_PAIRS_PER_TRIP = 8
FF_CHUNK = 1024
FF_UP_SLOTS = 3
FF_ROW_BLOCK = 64

BF16 = jnp.bfloat16
F32 = jnp.float32

V7X_VMEM_BYTES = 64 * 1024 * 1024
VMEM_REQUEST_CAP = V7X_VMEM_BYTES * 7 // 8
VMEM_TEMPORARIES = 1.5


def _vmem_limit(*buffers):
    need = sum(count * int(np.prod(shape)) * jnp.dtype(dtype).itemsize for shape, dtype, count in buffers)
    return min(VMEM_REQUEST_CAP, int(need * VMEM_TEMPORARIES))


def _dot(a, b):
    return jnp.dot(a, b, preferred_element_type=F32)


def _dot_nt(a, b):
    return lax.dot_general(a, b, (((1,), (1,)), ((), ())), preferred_element_type=F32)


def _split3(v):
    hi = v.astype(BF16).astype(F32)
    r1 = v - hi
    mid = r1.astype(BF16).astype(F32)
    lo = r1 - mid
    return hi, mid, lo


def _shift_rows(cur, prev8, k):
    ext = jnp.concatenate([prev8, cur], axis=0)
    return pltpu.roll(ext, k, axis=0)[SUBLANES:]


def _causal_conv3(cur, prev8, w):
    return (_shift_rows(cur, prev8, 2) * w[0:1] + _shift_rows(cur, prev8, 1) * w[1:2]
            + cur * w[2:3])


def _causal_conv3_ext(ext, w):
    return (pltpu.roll(ext, 2, axis=0)[SUBLANES:] * w[0:1] + pltpu.roll(ext, 1, axis=0)[SUBLANES:] * w[1:2]
            + ext[SUBLANES:] * w[2:3])


def _inproj_kernel(x_ref, gmix_ref, wg_ref, wk_ref, wvt_ref, bft_ref, sel_ref, m0_ref, m1_ref,
                   wconv_ref, gconv_ref, mixc_ref, qt_ref, k_ref, vt_ref, pcarry_ref, fcarry_ref,
                   *, n_heads, conv_ch):
    tm = x_ref.shape[1]
    attn_w = n_heads * HEAD_DIM
    gate_rows = fcarry_ref.shape[0]

    @pl.when(pl.program_id(1) == 0)
    def _():
        pcarry_ref[...] = jnp.zeros_like(pcarry_ref)
        fcarry_ref[...] = jnp.zeros_like(fcarry_ref)

    x = x_ref[0]
    ms = jnp.mean(x * x, axis=-1, keepdims=True)
    h = (x * lax.rsqrt(ms + EPS) * gmix_ref[...]).astype(BF16)

    vf = _dot_nt(wvt_ref[...], h)
    ones_rows = jnp.where(lax.broadcasted_iota(jnp.int32, (V_ROWS - HEAD_DIM, tm), 0) == 0, 1.0, 0.0)
    for hd in range(n_heads):
        vt_ref[0, hd] = jnp.concatenate(
            [vf[hd * HEAD_DIM:(hd + 1) * HEAD_DIM], ones_rows], axis=0).astype(BF16)

    zf = vf[2 * attn_w:2 * attn_w + gate_rows] + bft_ref[...]
    c = jnp.minimum(zf, 0.0) - jnp.log1p(jnp.exp(-jnp.abs(zf)))
    pos = lax.broadcasted_iota(jnp.int32, (gate_rows, tm), 1)
    d = 1
    while d < tm:
        c = c + jnp.where(pos >= d, pltpu.roll(c, d, axis=1), 0.0)
        d *= 2
    fcum = c + fcarry_ref[:, :1]
    fcarry_ref[...] = jnp.broadcast_to(fcum[:, tm - 1:tm], fcarry_ref.shape)

    f_hi, f_mid, f_lo = _split3(fcum * LOG2_E)
    pieces = jnp.concatenate(
        [f_hi, f_mid, f_lo, jnp.zeros((LANES - 3 * gate_rows, tm), F32)], axis=0).astype(BF16)

    zg = _dot(h, wg_ref[...])
    gb = zg[:, :conv_ch]
    p = zg[:, conv_ch:2 * conv_ch] * zg[:, 2 * conv_ch:]
    yc = gb * _causal_conv3(p, pcarry_ref[...], wconv_ref[...])
    pcarry_ref[...] = p[tm - SUBLANES:]
    msc = jnp.mean(yc * yc, axis=-1, keepdims=True)
    mixc_ref[0] = (yc * lax.rsqrt(msc + EPS) * gconv_ref[...]).astype(BF16)

    routed = _dot(sel_ref[...], pieces)

    scale = LOG2_E / (HEAD_DIM ** 0.5)
    extra_row = lax.broadcasted_iota(jnp.int32, (LANES - HEAD_DIM, tm), 0)
    for hd in range(n_heads):
        q_extra = jnp.where((extra_row >= 6 * hd) & (extra_row < 6 * hd + 3), routed,
                            jnp.where((extra_row >= 6 * hd + 3) & (extra_row < 6 * hd + 6), 1.0, 0.0))
        qt_ref[0, hd] = jnp.concatenate(
            [vf[attn_w + hd * HEAD_DIM:attn_w + (hd + 1) * HEAD_DIM] * scale, q_extra], axis=0).astype(BF16)

    extras = jnp.concatenate([jnp.zeros((HEAD_DIM, tm), F32), routed], axis=0).T
    lane = lax.broadcasted_iota(jnp.int32, (tm, LANES), 1)
    for grp in range(n_heads // 4):
        zk = _dot(h, wk_ref[:, 2 * grp * LANES:2 * (grp + 1) * LANES])
        for hd in range(4 * grp, 4 * grp + 4):
            kb = zk[:, ((hd // 2) % 2) * LANES:((hd // 2) % 2 + 1) * LANES]
            if hd % 2:
                kb = pltpu.roll(kb, HEAD_DIM, axis=1)
            m0 = m0_ref[hd:hd + 1, :]
            m1 = m1_ref[hd:hd + 1, :]
            k_ref[0, hd] = jnp.where(lane < HEAD_DIM, kb, extras * m1 + m0).astype(BF16)


def _inproj(x, g_mix, w_in, b_f, w_conv, g_conv_out, *, n_heads, conv_ch):
    b, s, d = x.shape
    attn_w = n_heads * HEAD_DIM
    tm = ROW_TILE_IN
    gate_rows = -(-n_heads // SUBLANES) * SUBLANES
    extra = LANES - HEAD_DIM
    assert s % tm == 0 and n_heads % 4 == 0 and 6 * n_heads <= extra and 3 * gate_rows <= LANES
    o3 = 3 * conv_ch
    wg = w_in[:, :o3].astype(BF16)
    wk = w_in[:, o3 + attn_w:o3 + 2 * attn_w].astype(BF16)
    wvt = jnp.concatenate(
        [w_in[:, o3 + 2 * attn_w:o3 + 3 * attn_w].T, w_in[:, o3:o3 + attn_w].T,
         jnp.pad(w_in[:, o3 + 3 * attn_w:].T, ((0, 2 * SUBLANES - n_heads), (0, 0)))], axis=0).astype(BF16)
    bft = jnp.broadcast_to(jnp.pad(b_f, (0, gate_rows - n_heads))[:, None], (gate_rows, tm))
    wconv = jnp.pad(w_conv, ((0, SUBLANES - CONV_K), (0, 0)))
    hh = np.arange(n_heads)
    sel = np.zeros((extra, LANES), np.float32)
    m0 = np.zeros((n_heads, LANES), np.float32)
    m1 = np.zeros((n_heads, LANES), np.float32)
    for j in range(3):
        sel[6 * hh + j, j * gate_rows + hh] = 1.0
        sel[6 * hh + 3 + j, j * gate_rows + hh] = -1.0
        m0[hh, HEAD_DIM + 6 * hh + j] = 1.0
        m1[hh, HEAD_DIM + 6 * hh + 3 + j] = 1.0

    const = lambda shape: pl.BlockSpec(shape, lambda bi, i: (0,) * len(shape),
                                       pipeline_mode=pl.Buffered(1))
    head_rows = pl.BlockSpec((1, n_heads, tm, LANES), lambda bi, i: (bi, 0, i, 0))
    return pl.pallas_call(
        functools.partial(_inproj_kernel, n_heads=n_heads, conv_ch=conv_ch),
        grid=(b, s // tm),
        in_specs=[
            pl.BlockSpec((1, tm, d), lambda bi, i: (bi, i, 0)),
            const((1, d)), const(wg.shape), const(wk.shape), const(wvt.shape), const(bft.shape),
            const(sel.shape), const(m0.shape), const(m1.shape), const(wconv.shape), const((1, conv_ch)),
        ],
        out_specs=[
            pl.BlockSpec((1, tm, conv_ch), lambda bi, i: (bi, i, 0)),
            pl.BlockSpec((1, n_heads, LANES, tm), lambda bi, i: (bi, 0, 0, i)),
            head_rows,
            pl.BlockSpec((1, n_heads, V_ROWS, tm), lambda bi, i: (bi, 0, 0, i)),
        ],
        out_shape=[
            jax.ShapeDtypeStruct((b, s, conv_ch), BF16),
            jax.ShapeDtypeStruct((b, n_heads, LANES, s), BF16),
            jax.ShapeDtypeStruct((b, n_heads, s, LANES), BF16),
            jax.ShapeDtypeStruct((b, n_heads, V_ROWS, s), BF16),
        ],
        scratch_shapes=[pltpu.VMEM((SUBLANES, conv_ch), F32), pltpu.VMEM((gate_rows, LANES), F32)],
        compiler_params=pltpu.CompilerParams(
            dimension_semantics=("arbitrary", "arbitrary"),
            vmem_limit_bytes=_vmem_limit(
                ((tm, d), F32, 2), (wg.shape, BF16, 1), (wk.shape, BF16, 1), (wvt.shape, BF16, 1),
                ((tm, conv_ch), BF16, 2), ((n_heads, 2 * LANES + V_ROWS, tm), BF16, 2))),
        name="inproj",
    )(x, g_mix.reshape(1, d), wg, wk, wvt, bft, jnp.asarray(sel, BF16), jnp.asarray(m0), jnp.asarray(m1),
      wconv, g_conv_out.reshape(1, conv_ch))


def _attn_kernel(qt_ref, k_ref, vt_ref, wa_ref, wb_ref, wc_ref, o_ref, wa_out, wb_out, wc_out,
                 s_ref, acc_ref):
    nh = k_ref.shape[1]
    s = k_ref.shape[2]
    tq, tk = ATTN_Q_TILE, ATTN_K_TILE
    r = tq // tk
    for w_in, w_out in ((wa_ref, wa_out), (wb_ref, wb_out), (wc_ref, wc_out)):
        w_out[...] = w_in[...].astype(w_out.dtype)

    def causal(d):
        shape = (tk, tq - d * tk)
        return lax.broadcasted_iota(jnp.int32, shape, 0) <= lax.broadcasted_iota(jnp.int32, shape, 1)

    def load_q(qi):
        q0 = pl.multiple_of(qi * tq, tq)
        return [qt_ref[0, hh, :, pl.ds(q0, tq)] for hh in range(nh)]

    def issue_scores(j, slot, qs):
        k0 = pl.multiple_of(j * tk, tk)
        maxima = []
        for hh in range(nh):
            st = _dot(k_ref[0, hh, pl.ds(k0, tk), :], qs[hh])
            s_ref[hh, slot] = st
            maxima.append(jnp.max(st, axis=0, keepdims=True))
        return tuple(maxima)

    def q_tile(qi, first_max):
        q0 = pl.multiple_of(qi * tq, tq)
        qs = load_q(qi)

        def consume(j, slot, ms, tile_max):
            k0 = pl.multiple_of(j * tk, tk)
            out = []
            for hh in range(nh):
                m_new = jnp.maximum(ms[hh], tile_max[hh])
                p = jnp.exp2(s_ref[hh, slot] - m_new).astype(BF16)
                alpha = jnp.exp2(ms[hh] - m_new)
                acc_ref[hh] = acc_ref[hh] * alpha + _dot(vt_ref[0, hh, :, pl.ds(k0, tk)], p)
                out.append(m_new)
            return tuple(out)

        def kv_pair(jj, state):
            ms, max0 = state
            max1 = issue_scores(2 * jj + 1, 1, qs)
            ms = consume(2 * jj, 0, ms, max0)
            max0 = issue_scores(2 * jj + 2, 0, qs)
            return consume(2 * jj + 1, 1, ms, max1), max0

        acc_ref[...] = jnp.zeros_like(acc_ref)
        state = (tuple(jnp.full((1, tq), NEG_BIG, F32) for _ in range(nh)), first_max)
        def pairs(j0, n, st):
            for u in range(n):
                st = kv_pair(j0 + u, st)
            return st

        n_pairs = qi * (r // 2)
        done = 0
        group = 1
        while group < ATTN_PAIRS_PER_TRIP:
            take = group * ((n_pairs // group) % 2)
            state = lax.cond(take > 0, functools.partial(pairs, done, group), lambda st: st, state)
            done = done + take
            group *= 2
        ms, _ = lax.fori_loop(
            0, n_pairs // ATTN_PAIRS_PER_TRIP,
            lambda i, st: pairs(done + ATTN_PAIRS_PER_TRIP * i, ATTN_PAIRS_PER_TRIP, st), state)

        def diag_cols(d):
            return slice(d * tk, tq), pl.multiple_of((r * qi + d) * tk, tk)

        def issue_diag(d):
            cols, k0 = diag_cols(d)
            for hh in range(nh):
                s_ref[hh, d % 2, :, cols] = _dot(k_ref[0, hh, pl.ds(k0, tk), :], qs[hh][:, d * tk:])

        def consume_diag(d, ms):
            cols, k0 = diag_cols(d)
            out = []
            for hh in range(nh):
                st = jnp.where(causal(d), s_ref[hh, d % 2, :, cols], NEG_BIG)
                m_old = ms[hh][:, cols]
                m_new = jnp.maximum(m_old, jnp.max(st, axis=0, keepdims=True))
                p = jnp.exp2(st - m_new).astype(BF16)
                acc_ref[hh, :, cols] = (acc_ref[hh, :, cols] * jnp.exp2(m_old - m_new)
                                        + _dot(vt_ref[0, hh, :, pl.ds(k0, tk)], p))
                out.append(m_new if d == 0 else jnp.concatenate([ms[hh][:, :d * tk], m_new], axis=1))
            return tuple(out)

        issue_diag(1)
        for d in range(r):
            ms = consume_diag(d, ms)
            if d + 2 < r:
                issue_diag(d + 2)
            elif d == r - 2:
                next_max = issue_scores(0, 0, load_q(jnp.minimum(qi + 1, s // tq - 1)))
        for hh in range(nh):
            acc = acc_ref[hh]
            o = acc[:HEAD_DIM] / acc[HEAD_DIM:HEAD_DIM + 1]
            o_ref[0, hh * HEAD_DIM:(hh + 1) * HEAD_DIM, pl.ds(q0, tq)] = o.astype(o_ref.dtype)
        return next_max

    lax.fori_loop(0, s // tq, q_tile, issue_scores(0, 0, load_q(0)))


def _attention(qt_aug, k_aug, vt_aug, weights):
    b, n_heads, s, _ = k_aug.shape
    nh = ATTN_HEADS_PER_STEP
    tq, tk = ATTN_Q_TILE, ATTN_K_TILE
    n_hsteps = n_heads // nh
    n_steps = b * n_hsteps
    bf16_rows = 2 * SUBLANES
    assert s % tq == 0 and tq % (2 * tk) == 0 and n_heads % nh == 0
    assert all(w.shape[0] % (n_steps * bf16_rows) == 0 for w in weights)
    rows = pl.BlockSpec((1, nh, s, LANES), lambda bi, hi: (bi, hi, 0, 0))
    slabs = [pl.BlockSpec((w.shape[0] // n_steps, w.shape[1]), lambda bi, hi: (bi * n_hsteps + hi, 0))
             for w in weights]
    slab_bytes = [((w.shape[0] // n_steps, w.shape[1]), F32, 3) for w in weights]
    out = pl.pallas_call(
        _attn_kernel,
        grid=(b, n_hsteps),
        in_specs=[pl.BlockSpec((1, nh, LANES, s), lambda bi, hi: (bi, hi, 0, 0)), rows,
                  pl.BlockSpec((1, nh, V_ROWS, s), lambda bi, hi: (bi, hi, 0, 0))] + slabs,
        out_specs=[pl.BlockSpec((1, nh * HEAD_DIM, s), lambda bi, hi: (bi, hi, 0))] + slabs,
        out_shape=[jax.ShapeDtypeStruct((b, n_heads * HEAD_DIM, s), BF16)]
                  + [jax.ShapeDtypeStruct(w.shape, BF16) for w in weights],
        scratch_shapes=[pltpu.VMEM((nh, ATTN_SLOTS, tk, tq), F32), pltpu.VMEM((nh, V_ROWS, tq), F32)],
        compiler_params=pltpu.CompilerParams(
            dimension_semantics=("arbitrary", "arbitrary"),
            vmem_limit_bytes=_vmem_limit(
                ((nh, 2 * LANES + V_ROWS + HEAD_DIM, s), BF16, 2),
                ((nh, ATTN_SLOTS, tk, tq), F32, 1), ((nh, V_ROWS, tq), F32, 1), *slab_bytes)),
        name="fox_attention",
    )(qt_aug, k_aug, vt_aug, *weights)
    return out[0], out[1:]


def _post_kernel(x_ref, mixc_ref, yt_ref, gattn_ref, wo_ref, gffn_ref, wup_ref, wfc_ref, wdown_ref,
                 gfin_ref, out_ref, ucarry_ref, h2_ref, up_ref, act_ref, *, d_ff, final_norm):
    tm = x_ref.shape[1]

    @pl.when(pl.program_id(1) == 0)
    def _():
        ucarry_ref[...] = jnp.zeros_like(ucarry_ref)

    yt = yt_ref[0].astype(F32)
    msa = jnp.mean(yt * yt, axis=0, keepdims=True)
    yn = (yt * lax.rsqrt(msa + EPS) * gattn_ref[...]).T.astype(BF16)
    mix = jnp.concatenate([mixc_ref[0], yn], axis=1)
    x1 = x_ref[0] + _dot(mix, wo_ref[...])

    ms = jnp.mean(x1 * x1, axis=-1, keepdims=True)
    h2 = (x1 * lax.rsqrt(ms + EPS) * gffn_ref[...]).astype(BF16)

    out_ref[0] = x1
    h2_ref[...] = h2
    starts = list(range(0, d_ff, FF_CHUNK))
    n_chunks = len(starts)
    width = lambda c: min(FF_CHUNK, d_ff - starts[c])
    cols = lambda c: (slice(starts[c], starts[c] + width(c)),
                      slice(d_ff + starts[c], d_ff + starts[c] + width(c)))

    def issue_up(c):
        w = width(c)
        for half, cc in enumerate(cols(c)):
            up_ref[c % FF_UP_SLOTS, half, :SUBLANES, :w] = ucarry_ref[:, cc]
            up_ref[c % FF_UP_SLOTS, half, SUBLANES:, :w] = _dot(h2_ref[...], wup_ref[:, cc])

    def gate_and_down(c):
        ca, cg = cols(c)
        w = width(c)
        slot = c % FF_UP_SLOTS
        for r0 in range(0, tm, FF_ROW_BLOCK):
            ext = slice(r0, r0 + FF_ROW_BLOCK + SUBLANES)
            a = _causal_conv3_ext(up_ref[slot, 0, ext, :w], wfc_ref[:, ca])
            g = _causal_conv3_ext(up_ref[slot, 1, ext, :w], wfc_ref[:, cg])
            act_ref[c % 2, r0:r0 + FF_ROW_BLOCK, :w] = (g * jax.nn.sigmoid(g) * a).astype(BF16)
        ucarry_ref[:, ca] = up_ref[slot, 0, tm:, :w]
        ucarry_ref[:, cg] = up_ref[slot, 1, tm:, :w]
        acc = out_ref[0] + _dot(act_ref[c % 2, :, :w], wdown_ref[ca, :])
        if final_norm and c == n_chunks - 1:
            msf = jnp.mean(acc * acc, axis=-1, keepdims=True)
            acc = acc * lax.rsqrt(msf + EPS) * gfin_ref[...]
        out_ref[0] = acc

    for c in range(min(FF_UP_SLOTS - 1, n_chunks)):
        issue_up(c)
    for c in range(n_chunks):
        if c + FF_UP_SLOTS - 1 < n_chunks:
            issue_up(c + FF_UP_SLOTS - 1)
        gate_and_down(c)


def _post(x, mixc, yt, g_attn_out, w_o, g_ffn, w_up, w_ffn_conv, w_down, g_final, *, final_norm):
    b, s, d = x.shape
    conv_ch = mixc.shape[-1]
    attn_w = yt.shape[1]
    d_ff = w_down.shape[0]
    tm = ROW_TILE_POST
    assert s % tm == 0 and d_ff % LANES == 0 and FF_CHUNK % LANES == 0
    assert w_o.dtype == w_up.dtype == w_down.dtype == BF16
    wfc = jnp.pad(w_ffn_conv, ((0, SUBLANES - CONV_K), (0, 0)))

    const = lambda shape: pl.BlockSpec(shape, lambda bi, i: (0,) * len(shape),
                                       pipeline_mode=pl.Buffered(1))
    return pl.pallas_call(
        functools.partial(_post_kernel, d_ff=d_ff, final_norm=final_norm),
        grid=(b, s // tm),
        in_specs=[
            pl.BlockSpec((1, tm, d), lambda bi, i: (bi, i, 0)),
            pl.BlockSpec((1, tm, conv_ch), lambda bi, i: (bi, i, 0)),
            pl.BlockSpec((1, attn_w, tm), lambda bi, i: (bi, 0, i)),
            const((attn_w, 1)), const(w_o.shape), const((1, d)), const(w_up.shape),
            const(wfc.shape), const(w_down.shape), const((1, d)),
        ],
        out_specs=pl.BlockSpec((1, tm, d), lambda bi, i: (bi, i, 0)),
        out_shape=jax.ShapeDtypeStruct((b, s, d), F32),
        scratch_shapes=[pltpu.VMEM((SUBLANES, 2 * d_ff), F32), pltpu.VMEM((tm, d), BF16),
                        pltpu.VMEM((FF_UP_SLOTS, 2, tm + SUBLANES, FF_CHUNK), F32),
                        pltpu.VMEM((2, tm, FF_CHUNK), BF16)],
        compiler_params=pltpu.CompilerParams(
            dimension_semantics=("arbitrary", "arbitrary"),
            vmem_limit_bytes=_vmem_limit(
                ((tm, d), F32, 4), ((tm, conv_ch + attn_w), BF16, 2),
                (w_o.shape, BF16, 1), (w_up.shape, BF16, 1), (w_down.shape, BF16, 1),
                ((tm, d), BF16, 1), ((FF_UP_SLOTS, 2, tm + SUBLANES, FF_CHUNK), F32, 1),
                ((2, tm, FF_CHUNK), BF16, 1))),
        name="post_ffn",
    )(x, mixc, yt, g_attn_out.reshape(attn_w, 1), w_o, g_ffn.reshape(1, d), w_up, wfc, w_down,
      g_final.reshape(1, d))


def kernel(x, g_mix, w_in, b_f, w_conv, g_conv_out, g_attn_out, w_o, g_ffn, w_up, w_ffn_conv, w_down,
           g_final):
    depth = w_in.shape[0]
    conv_ch = w_conv.shape[-1]
    n_heads = b_f.shape[-1]
    for l in range(depth):
        mixc, qt_aug, k_aug, vt_aug = _inproj(x, g_mix[l], w_in[l], b_f[l], w_conv[l], g_conv_out[l],
                                             n_heads=n_heads, conv_ch=conv_ch)
        yt, (w_o_b, w_up_b, w_down_b) = _attention(qt_aug, k_aug, vt_aug, (w_o[l], w_up[l], w_down[l]))
        x = _post(x, mixc, yt, g_attn_out[l], w_o_b, g_ffn[l], w_up_b, w_ffn_conv[l], w_down_b,
                  g_final, final_norm=(l == depth - 1))
    return x
```

```python
import functools

import jax
import jax.numpy as jnp
import numpy as np
from jax import lax
from jax.experimental import pallas as pl
from jax.experimental.pallas import tpu as pltpu

EPS = 1e-6
CONV_K = 3
HEAD_DIM = 64
LANES = 128
SUBLANES = 8
NEG_BIG = -1e30
LOG2_E = 1.4426950408889634
V_ROWS = HEAD_DIM + 16

ROW_TILE_IN = 1024
ROW_TILE_POST = 512
ATTN_Q_TILE = 512
ATTN_K_TILE = 256
ATTN_HEADS_PER_STEP = 2
ATTN_SLOTS = 2
ATTN_PAIRS_PER_TRIP = 8
FF_CHUNK = 1024
FF_UP_SLOTS = 3
FF_ROW_BLOCK = 64

BF16 = jnp.bfloat16
F32 = jnp.float32

V7X_VMEM_BYTES = 64 * 1024 * 1024
VMEM_REQUEST_CAP = V7X_VMEM_BYTES * 7 // 8
VMEM_TEMPORARIES = 1.5


def _vmem_limit(*buffers):
    need = sum(count * int(np.prod(shape)) * jnp.dtype(dtype).itemsize for shape, dtype, count in buffers)
    return min(VMEM_REQUEST_CAP, int(need * VMEM_TEMPORARIES))


def _dot(a, b):
    return jnp.dot(a, b, preferred_element_type=F32)


def _dot_nt(a, b):
    return lax.dot_general(a, b, (((1,), (1,)), ((), ())), preferred_element_type=F32)


def _split3(v):
    hi = v.astype(BF16).astype(F32)
    r1 = v - hi
    mid = r1.astype(BF16).astype(F32)
    lo = r1 - mid
    return hi, mid, lo


def _shift_rows(cur, prev8, k):
    ext = jnp.concatenate([prev8, cur], axis=0)
    return pltpu.roll(ext, k, axis=0)[SUBLANES:]


def _causal_conv3(cur, prev8, w):
    return (_shift_rows(cur, prev8, 2) * w[0:1] + _shift_rows(cur, prev8, 1) * w[1:2]
            + cur * w[2:3])


def _causal_conv3_ext(ext, w):
    return (pltpu.roll(ext, 2, axis=0)[SUBLANES:] * w[0:1] + pltpu.roll(ext, 1, axis=0)[SUBLANES:] * w[1:2]
            + ext[SUBLANES:] * w[2:3])


def _inproj_kernel(x_ref, gmix_ref, win_ref, wft_ref, bft_ref, sel_ref, m0_ref, m1_ref,
                   wconv_ref, gconv_ref, mixc_ref, qt_ref, k_ref, vt_ref, pcarry_ref, fcarry_ref,
                   wg_ref, wk_ref, wvt_ref, *, n_heads, conv_ch):
    tm = x_ref.shape[1]
    attn_w = n_heads * HEAD_DIM
    gate_rows = fcarry_ref.shape[0]

    @pl.when((pl.program_id(0) == 0) & (pl.program_id(1) == 0))
    def _():
        o3 = 3 * conv_ch
        wg_ref[...] = win_ref[:, :o3].astype(BF16)
        wk_ref[...] = win_ref[:, o3 + attn_w:o3 + 2 * attn_w].astype(BF16)
        wvt_ref[:attn_w] = win_ref[:, o3 + 2 * attn_w:o3 + 3 * attn_w].T.astype(BF16)
        wvt_ref[attn_w:2 * attn_w] = win_ref[:, o3:o3 + attn_w].T.astype(BF16)
        wvt_ref[2 * attn_w:] = wft_ref[...].astype(BF16)

    @pl.when(pl.program_id(1) == 0)
    def _():
        pcarry_ref[...] = jnp.zeros_like(pcarry_ref)
        fcarry_ref[...] = jnp.zeros_like(fcarry_ref)

    x = x_ref[0]
    ms = jnp.mean(x * x, axis=-1, keepdims=True)
    h = (x * lax.rsqrt(ms + EPS) * gmix_ref[...]).astype(BF16)

    vf = _dot_nt(wvt_ref[...], h)
    ones_rows = jnp.where(lax.broadcasted_iota(jnp.int32, (V_ROWS - HEAD_DIM, tm), 0) == 0, 1.0, 0.0)
    for hd in range(n_heads):
        vt_ref[0, hd] = jnp.concatenate(
            [vf[hd * HEAD_DIM:(hd + 1) * HEAD_DIM], ones_rows], axis=0).astype(BF16)

    zf = vf[2 * attn_w:2 * attn_w + gate_rows] + bft_ref[...]
    c = jnp.minimum(zf, 0.0) - jnp.log1p(jnp.exp(-jnp.abs(zf)))
    pos = lax.broadcasted_iota(jnp.int32, (gate_rows, tm), 1)
    d = 1
    while d < tm:
        c = c + jnp.where(pos >= d, pltpu.roll(c, d, axis=1), 0.0)
        d *= 2
    fcum = c + fcarry_ref[:, :1]
    fcarry_ref[...] = jnp.broadcast_to(fcum[:, tm - 1:tm], fcarry_ref.shape)

    f_hi, f_mid, f_lo = _split3(fcum * LOG2_E)
    pieces = jnp.concatenate(
        [f_hi, f_mid, f_lo, jnp.zeros((LANES - 3 * gate_rows, tm), F32)], axis=0).astype(BF16)

    zg = _dot(h, wg_ref[...])
    gb = zg[:, :conv_ch]
    p = zg[:, conv_ch:2 * conv_ch] * zg[:, 2 * conv_ch:]
    yc = gb * _causal_conv3(p, pcarry_ref[...], wconv_ref[...])
    pcarry_ref[...] = p[tm - SUBLANES:]
    msc = jnp.mean(yc * yc, axis=-1, keepdims=True)
    mixc_ref[0] = (yc * lax.rsqrt(msc + EPS) * gconv_ref[...]).astype(BF16)

    routed = _dot(sel_ref[...], pieces)

    scale = LOG2_E / (HEAD_DIM ** 0.5)
    extra_row = lax.broadcasted_iota(jnp.int32, (LANES - HEAD_DIM, tm), 0)
    for hd in range(n_heads):
        q_extra = jnp.where((extra_row >= 6 * hd) & (extra_row < 6 * hd + 3), routed,
                            jnp.where((extra_row >= 6 * hd + 3) & (extra_row < 6 * hd + 6), 1.0, 0.0))
        qt_ref[0, hd] = jnp.concatenate(
            [vf[attn_w + hd * HEAD_DIM:attn_w + (hd + 1) * HEAD_DIM] * scale, q_extra], axis=0).astype(BF16)

    extras = jnp.concatenate([jnp.zeros((HEAD_DIM, tm), F32), routed], axis=0).T
    lane = lax.broadcasted_iota(jnp.int32, (tm, LANES), 1)
    for grp in range(n_heads // 4):
        zk = _dot(h, wk_ref[:, 2 * grp * LANES:2 * (grp + 1) * LANES])
        for hd in range(4 * grp, 4 * grp + 4):
            kb = zk[:, ((hd // 2) % 2) * LANES:((hd // 2) % 2 + 1) * LANES]
            if hd % 2:
                kb = pltpu.roll(kb, HEAD_DIM, axis=1)
            m0 = m0_ref[hd:hd + 1, :]
            m1 = m1_ref[hd:hd + 1, :]
            k_ref[0, hd] = jnp.where(lane < HEAD_DIM, kb, extras * m1 + m0).astype(BF16)


def _inproj(x, g_mix, w_in, b_f, w_conv, g_conv_out, *, n_heads, conv_ch):
    b, s, d = x.shape
    attn_w = n_heads * HEAD_DIM
    tm = ROW_TILE_IN
    gate_rows = -(-n_heads // SUBLANES) * SUBLANES
    extra = LANES - HEAD_DIM
    assert s % tm == 0 and n_heads % 4 == 0 and 6 * n_heads <= extra and 3 * gate_rows <= LANES
    o3 = 3 * conv_ch
    wft = jnp.pad(w_in[:, o3 + 3 * attn_w:].T, ((0, 2 * SUBLANES - n_heads), (0, 0)))
    wg_shape, wk_shape, wvt_shape = (d, o3), (d, attn_w), (2 * attn_w + 2 * SUBLANES, d)
    bft = jnp.broadcast_to(jnp.pad(b_f, (0, gate_rows - n_heads))[:, None], (gate_rows, tm))
    wconv = jnp.pad(w_conv, ((0, SUBLANES - CONV_K), (0, 0)))
    hh = np.arange(n_heads)
    sel = np.zeros((extra, LANES), np.float32)
    m0 = np.zeros((n_heads, LANES), np.float32)
    m1 = np.zeros((n_heads, LANES), np.float32)
    for j in range(3):
        sel[6 * hh + j, j * gate_rows + hh] = 1.0
        sel[6 * hh + 3 + j, j * gate_rows + hh] = -1.0
        m0[hh, HEAD_DIM + 6 * hh + j] = 1.0
        m1[hh, HEAD_DIM + 6 * hh + 3 + j] = 1.0

    const = lambda shape: pl.BlockSpec(shape, lambda bi, i: (0,) * len(shape),
                                       pipeline_mode=pl.Buffered(1))
    head_rows = pl.BlockSpec((1, n_heads, tm, LANES), lambda bi, i: (bi, 0, i, 0))
    return pl.pallas_call(
        functools.partial(_inproj_kernel, n_heads=n_heads, conv_ch=conv_ch),
        grid=(b, s // tm),
        in_specs=[
            pl.BlockSpec((1, tm, d), lambda bi, i: (bi, i, 0)),
            const((1, d)), const(w_in.shape), const(wft.shape), const(bft.shape),
            const(sel.shape), const(m0.shape), const(m1.shape), const(wconv.shape), const((1, conv_ch)),
        ],
        out_specs=[
            pl.BlockSpec((1, tm, conv_ch), lambda bi, i: (bi, i, 0)),
            pl.BlockSpec((1, n_heads, LANES, tm), lambda bi, i: (bi, 0, 0, i)),
            head_rows,
            pl.BlockSpec((1, n_heads, V_ROWS, tm), lambda bi, i: (bi, 0, 0, i)),
        ],
        out_shape=[
            jax.ShapeDtypeStruct((b, s, conv_ch), BF16),
            jax.ShapeDtypeStruct((b, n_heads, LANES, s), BF16),
            jax.ShapeDtypeStruct((b, n_heads, s, LANES), BF16),
            jax.ShapeDtypeStruct((b, n_heads, V_ROWS, s), BF16),
        ],
        scratch_shapes=[pltpu.VMEM((SUBLANES, conv_ch), F32), pltpu.VMEM((gate_rows, LANES), F32),
                        pltpu.VMEM(wg_shape, BF16), pltpu.VMEM(wk_shape, BF16), pltpu.VMEM(wvt_shape, BF16)],
        compiler_params=pltpu.CompilerParams(
            dimension_semantics=("arbitrary", "arbitrary"),
            vmem_limit_bytes=_vmem_limit(
                ((tm, d), F32, 2), (w_in.shape, F32, 1), (wg_shape, BF16, 1), (wk_shape, BF16, 1),
                (wvt_shape, BF16, 1),
                ((tm, conv_ch), BF16, 2), ((n_heads, 2 * LANES + V_ROWS, tm), BF16, 2))),
        name="inproj",
    )(x, g_mix.reshape(1, d), w_in, wft, bft, jnp.asarray(sel, BF16), jnp.asarray(m0), jnp.asarray(m1),
      wconv, g_conv_out.reshape(1, conv_ch))


def _attn_kernel(qt_ref, k_ref, vt_ref, wa_ref, wb_ref, wc_ref, o_ref, wa_out, wb_out, wc_out,
                 s_ref, acc_ref):
    nh = k_ref.shape[1]
    s = k_ref.shape[2]
    tq, tk = ATTN_Q_TILE, ATTN_K_TILE
    r = tq // tk
    for w_in, w_out in ((wa_ref, wa_out), (wb_ref, wb_out), (wc_ref, wc_out)):
        w_out[...] = w_in[...].astype(w_out.dtype)

    def causal(d):
        shape = (tk, tq - d * tk)
        return lax.broadcasted_iota(jnp.int32, shape, 0) <= lax.broadcasted_iota(jnp.int32, shape, 1)

    def load_q(qi):
        q0 = pl.multiple_of(qi * tq, tq)
        return [qt_ref[0, hh, :, pl.ds(q0, tq)] for hh in range(nh)]

    def issue_scores(j, slot, qs):
        k0 = pl.multiple_of(j * tk, tk)
        maxima = []
        for hh in range(nh):
            st = _dot(k_ref[0, hh, pl.ds(k0, tk), :], qs[hh])
            s_ref[hh, slot] = st
            maxima.append(jnp.max(st, axis=0, keepdims=True))
        return tuple(maxima)

    def q_tile(qi, first_max):
        q0 = pl.multiple_of(qi * tq, tq)
        qs = load_q(qi)

        def consume(j, slot, ms, tile_max):
            k0 = pl.multiple_of(j * tk, tk)
            out = []
            for hh in range(nh):
                m_new = jnp.maximum(ms[hh], tile_max[hh])
                p = jnp.exp2(s_ref[hh, slot] - m_new).astype(BF16)
                alpha = jnp.exp2(ms[hh] - m_new)
                acc_ref[hh] = acc_ref[hh] * alpha + _dot(vt_ref[0, hh, :, pl.ds(k0, tk)], p)
                out.append(m_new)
            return tuple(out)

        def kv_pair(jj, state):
            ms, max0 = state
            max1 = issue_scores(2 * jj + 1, 1, qs)
            ms = consume(2 * jj, 0, ms, max0)
            max0 = issue_scores(2 * jj + 2, 0, qs)
            return consume(2 * jj + 1, 1, ms, max1), max0

        acc_ref[...] = jnp.zeros_like(acc_ref)
        state = (tuple(jnp.full((1, tq), NEG_BIG, F32) for _ in range(nh)), first_max)
        def pairs(j0, n, st):
            for u in range(n):
                st = kv_pair(j0 + u, st)
            return st

        n_pairs = qi * (r // 2)
        done = 0
        group = 1
        while group < ATTN_PAIRS_PER_TRIP:
            take = group * ((n_pairs // group) % 2)
            state = lax.cond(take > 0, functools.partial(pairs, done, group), lambda st: st, state)
            done = done + take
            group *= 2
        ms, _ = lax.fori_loop(
            0, n_pairs // ATTN_PAIRS_PER_TRIP,
            lambda i, st: pairs(done + ATTN_PAIRS_PER_TRIP * i, ATTN_PAIRS_PER_TRIP, st), state)

        def diag_cols(d):
            return slice(d * tk, tq), pl.multiple_of((r * qi + d) * tk, tk)

        def issue_diag(d):
            cols, k0 = diag_cols(d)
            for hh in range(nh):
                s_ref[hh, d % 2, :, cols] = _dot(k_ref[0, hh, pl.ds(k0, tk), :], qs[hh][:, d * tk:])

        def consume_diag(d, ms):
            cols, k0 = diag_cols(d)
            out = []
            for hh in range(nh):
                st = jnp.where(causal(d), s_ref[hh, d % 2, :, cols], NEG_BIG)
                m_old = ms[hh][:, cols]
                m_new = jnp.maximum(m_old, jnp.max(st, axis=0, keepdims=True))
                p = jnp.exp2(st - m_new).astype(BF16)
                acc_ref[hh, :, cols] = (acc_ref[hh, :, cols] * jnp.exp2(m_old - m_new)
                                        + _dot(vt_ref[0, hh, :, pl.ds(k0, tk)], p))
                out.append(m_new if d == 0 else jnp.concatenate([ms[hh][:, :d * tk], m_new], axis=1))
            return tuple(out)

        issue_diag(1)
        for d in range(r):
            ms = consume_diag(d, ms)
            if d + 2 < r:
                issue_diag(d + 2)
            elif d == r - 2:
                next_max = issue_scores(0, 0, load_q(jnp.minimum(qi + 1, s // tq - 1)))
        for hh in range(nh):
            acc = acc_ref[hh]
            o = acc[:HEAD_DIM] / acc[HEAD_DIM:HEAD_DIM + 1]
            o_ref[0, hh * HEAD_DIM:(hh + 1) * HEAD_DIM, pl.ds(q0, tq)] = o.astype(o_ref.dtype)
        return next_max

    lax.fori_loop(0, s // tq, q_tile, issue_scores(0, 0, load_q(0)))


def _attention(qt_aug, k_aug, vt_aug, weights):
    b, n_heads, s, _ = k_aug.shape
    nh = ATTN_HEADS_PER_STEP
    tq, tk = ATTN_Q_TILE, ATTN_K_TILE
    n_hsteps = n_heads // nh
    n_steps = b * n_hsteps
    bf16_rows = 2 * SUBLANES
    assert s % tq == 0 and tq % (2 * tk) == 0 and n_heads % nh == 0
    assert all(w.shape[0] % (n_steps * bf16_rows) == 0 for w in weights)
    rows = pl.BlockSpec((1, nh, s, LANES), lambda bi, hi: (bi, hi, 0, 0))
    slabs = [pl.BlockSpec((w.shape[0] // n_steps, w.shape[1]), lambda bi, hi: (bi * n_hsteps + hi, 0))
             for w in weights]
    slab_bytes = [((w.shape[0] // n_steps, w.shape[1]), F32, 3) for w in weights]
    out = pl.pallas_call(
        _attn_kernel,
        grid=(b, n_hsteps),
        in_specs=[pl.BlockSpec((1, nh, LANES, s), lambda bi, hi: (bi, hi, 0, 0)), rows,
                  pl.BlockSpec((1, nh, V_ROWS, s), lambda bi, hi: (bi, hi, 0, 0))] + slabs,
        out_specs=[pl.BlockSpec((1, nh * HEAD_DIM, s), lambda bi, hi: (bi, hi, 0))] + slabs,
        out_shape=[jax.ShapeDtypeStruct((b, n_heads * HEAD_DIM, s), BF16)]
                  + [jax.ShapeDtypeStruct(w.shape, BF16) for w in weights],
        scratch_shapes=[pltpu.VMEM((nh, ATTN_SLOTS, tk, tq), F32), pltpu.VMEM((nh, V_ROWS, tq), F32)],
        compiler_params=pltpu.CompilerParams(
            dimension_semantics=("arbitrary", "arbitrary"),
            vmem_limit_bytes=_vmem_limit(
                ((nh, 2 * LANES + V_ROWS + HEAD_DIM, s), BF16, 2),
                ((nh, ATTN_SLOTS, tk, tq), F32, 1), ((nh, V_ROWS, tq), F32, 1), *slab_bytes)),
        name="fox_attention",
    )(qt_aug, k_aug, vt_aug, *weights)
    return out[0], out[1:]


def _post_kernel(x_ref, mixc_ref, yt_ref, gattn_ref, wo_ref, gffn_ref, wup_ref, wfc_ref, wdown_ref,
                 gfin_ref, out_ref, ucarry_ref, h2_ref, up_ref, act_ref, *, d_ff, final_norm):
    tm = x_ref.shape[1]

    @pl.when(pl.program_id(1) == 0)
    def _():
        ucarry_ref[...] = jnp.zeros_like(ucarry_ref)

    yt = yt_ref[0].astype(F32)
    msa = jnp.mean(yt * yt, axis=0, keepdims=True)
    yn = (yt * lax.rsqrt(msa + EPS) * gattn_ref[...]).T.astype(BF16)
    mix = jnp.concatenate([mixc_ref[0], yn], axis=1)
    x1 = x_ref[0] + _dot(mix, wo_ref[...])

    ms = jnp.mean(x1 * x1, axis=-1, keepdims=True)
    h2 = (x1 * lax.rsqrt(ms + EPS) * gffn_ref[...]).astype(BF16)

    out_ref[0] = x1
    h2_ref[...] = h2
    starts = list(range(0, d_ff, FF_CHUNK))
    n_chunks = len(starts)
    width = lambda c: min(FF_CHUNK, d_ff - starts[c])
    cols = lambda c: (slice(starts[c], starts[c] + width(c)),
                      slice(d_ff + starts[c], d_ff + starts[c] + width(c)))

    def issue_up(c):
        w = width(c)
        for half, cc in enumerate(cols(c)):
            up_ref[c % FF_UP_SLOTS, half, :SUBLANES, :w] = ucarry_ref[:, cc]
            up_ref[c % FF_UP_SLOTS, half, SUBLANES:, :w] = _dot(h2_ref[...], wup_ref[:, cc])

    def gate_and_down(c):
        ca, cg = cols(c)
        w = width(c)
        slot = c % FF_UP_SLOTS
        for r0 in range(0, tm, FF_ROW_BLOCK):
            ext = slice(r0, r0 + FF_ROW_BLOCK + SUBLANES)
            a = _causal_conv3_ext(up_ref[slot, 0, ext, :w], wfc_ref[:, ca])
            g = _causal_conv3_ext(up_ref[slot, 1, ext, :w], wfc_ref[:, cg])
            act_ref[c % 2, r0:r0 + FF_ROW_BLOCK, :w] = (g * jax.nn.sigmoid(g) * a).astype(BF16)
        ucarry_ref[:, ca] = up_ref[slot, 0, tm:, :w]
        ucarry_ref[:, cg] = up_ref[slot, 1, tm:, :w]
        acc = out_ref[0] + _dot(act_ref[c % 2, :, :w], wdown_ref[ca, :])
        if final_norm and c == n_chunks - 1:
            msf = jnp.mean(acc * acc, axis=-1, keepdims=True)
            acc = acc * lax.rsqrt(msf + EPS) * gfin_ref[...]
        out_ref[0] = acc

    for c in range(min(FF_UP_SLOTS - 1, n_chunks)):
        issue_up(c)
    for c in range(n_chunks):
        if c + FF_UP_SLOTS - 1 < n_chunks:
            issue_up(c + FF_UP_SLOTS - 1)
        gate_and_down(c)


def _post(x, mixc, yt, g_attn_out, w_o, g_ffn, w_up, w_ffn_conv, w_down, g_final, *, final_norm):
    b, s, d = x.shape
    conv_ch = mixc.shape[-1]
    attn_w = yt.shape[1]
    d_ff = w_down.shape[0]
    tm = ROW_TILE_POST
    assert s % tm == 0 and d_ff % LANES == 0 and FF_CHUNK % LANES == 0
    assert w_o.dtype == w_up.dtype == w_down.dtype == BF16
    wfc = jnp.pad(w_ffn_conv, ((0, SUBLANES - CONV_K), (0, 0)))

    const = lambda shape: pl.BlockSpec(shape, lambda bi, i: (0,) * len(shape),
                                       pipeline_mode=pl.Buffered(1))
    return pl.pallas_call(
        functools.partial(_post_kernel, d_ff=d_ff, final_norm=final_norm),
        grid=(b, s // tm),
        in_specs=[
            pl.BlockSpec((1, tm, d), lambda bi, i: (bi, i, 0)),
            pl.BlockSpec((1, tm, conv_ch), lambda bi, i: (bi, i, 0)),
            pl.BlockSpec((1, attn_w, tm), lambda bi, i: (bi, 0, i)),
            const((attn_w, 1)), const(w_o.shape), const((1, d)), const(w_up.shape),
            const(wfc.shape), const(w_down.shape), const((1, d)),
        ],
        out_specs=pl.BlockSpec((1, tm, d), lambda bi, i: (bi, i, 0)),
        out_shape=jax.ShapeDtypeStruct((b, s, d), F32),
        scratch_shapes=[pltpu.VMEM((SUBLANES, 2 * d_ff), F32), pltpu.VMEM((tm, d), BF16),
                        pltpu.VMEM((FF_UP_SLOTS, 2, tm + SUBLANES, FF_CHUNK), F32),
                        pltpu.VMEM((2, tm, FF_CHUNK), BF16)],
        compiler_params=pltpu.CompilerParams(
            dimension_semantics=("arbitrary", "arbitrary"),
            vmem_limit_bytes=_vmem_limit(
                ((tm, d), F32, 4), ((tm, conv_ch + attn_w), BF16, 2),
                (w_o.shape, BF16, 1), (w_up.shape, BF16, 1), (w_down.shape, BF16, 1),
                ((tm, d), BF16, 1), ((FF_UP_SLOTS, 2, tm + SUBLANES, FF_CHUNK), F32, 1),
                ((2, tm, FF_CHUNK), BF16, 1))),
        name="post_ffn",
    )(x, mixc, yt, g_attn_out.reshape(attn_w, 1), w_o, g_ffn.reshape(1, d), w_up, wfc, w_down,
      g_final.reshape(1, d))


def kernel(x, g_mix, w_in, b_f, w_conv, g_conv_out, g_attn_out, w_o, g_ffn, w_up, w_ffn_conv, w_down,
           g_final):
    depth = w_in.shape[0]
    conv_ch = w_conv.shape[-1]
    n_heads = b_f.shape[-1]
    for l in range(depth):
        mixc, qt_aug, k_aug, vt_aug = _inproj(x, g_mix[l], w_in[l], b_f[l], w_conv[l], g_conv_out[l],
                                             n_heads=n_heads, conv_ch=conv_ch)
        yt, (w_o_b, w_up_b, w_down_b) = _attention(qt_aug, k_aug, vt_aug, (w_o[l], w_up[l], w_down[l]))
        x = _post(x, mixc, yt, g_attn_out[l], w_o_b, g_ffn[l], w_up_b, w_ffn_conv[l], w_down_b,
                  g_final, final_norm=(l == depth - 1))
    return x
```

```python
import functools

import jax
import jax.numpy as jnp
import numpy as np
from jax import lax
from jax.experimental import pallas as pl
from jax.experimental.pallas import tpu as pltpu

EPS = 1e-6
CONV_K = 3
HEAD_DIM = 64
LANES = 128
SUBLANES = 8
NEG_BIG = -1e30
LOG2_E = 1.4426950408889634
V_ROWS = HEAD_DIM + 16

ROW_TILE_IN = 1024
ROW_TILE_POST = 512
ATTN_Q_TILE = 512
ATTN_K_TILE = 256
ATTN_HEADS_PER_STEP = 2
ATTN_SLOTS = 2
ATTN_PAIRS_PER_TRIP = 8
FF_CHUNK = 1024
FF_UP_SLOTS = 3
FF_ROW_BLOCK = 64

BF16 = jnp.bfloat16
F32 = jnp.float32

V7X_VMEM_BYTES = 64 * 1024 * 1024
VMEM_REQUEST_CAP = V7X_VMEM_BYTES * 7 // 8
VMEM_TEMPORARIES = 1.5


def _vmem_limit(*buffers):
    need = sum(count * int(np.prod(shape)) * jnp.dtype(dtype).itemsize for shape, dtype, count in buffers)
    return min(VMEM_REQUEST_CAP, int(need * VMEM_TEMPORARIES))


def _dot(a, b):
    return jnp.dot(a, b, preferred_element_type=F32)


def _dot_nt(a, b):
    return lax.dot_general(a, b, (((1,), (1,)), ((), ())), preferred_element_type=F32)


def _split3(v):
    hi = v.astype(BF16).astype(F32)
    r1 = v - hi
    mid = r1.astype(BF16).astype(F32)
    lo = r1 - mid
    return hi, mid, lo


def _shift_rows(cur, prev8, k):
    ext = jnp.concatenate([prev8, cur], axis=0)
    return pltpu.roll(ext, k, axis=0)[SUBLANES:]


def _causal_conv3(cur, prev8, w):
    return (_shift_rows(cur, prev8, 2) * w[0:1] + _shift_rows(cur, prev8, 1) * w[1:2]
            + cur * w[2:3])


def _causal_conv3_ext(ext, w):
    return (pltpu.roll(ext, 2, axis=0)[SUBLANES:] * w[0:1] + pltpu.roll(ext, 1, axis=0)[SUBLANES:] * w[1:2]
            + ext[SUBLANES:] * w[2:3])


def _inproj_kernel(x_ref, gmix_ref, wg_ref, wk_ref, wvt_ref, bft_ref, sel_ref, m0_ref, m1_ref,
                   wconv_ref, gconv_ref, mixc_ref, qt_ref, k_ref, vt_ref, pcarry_ref, fcarry_ref,
                   *, n_heads, conv_ch):
    tm = x_ref.shape[1]
    attn_w = n_heads * HEAD_DIM
    gate_rows = fcarry_ref.shape[0]

    @pl.when(pl.program_id(1) == 0)
    def _():
        pcarry_ref[...] = jnp.zeros_like(pcarry_ref)
        fcarry_ref[...] = jnp.zeros_like(fcarry_ref)

    x = x_ref[0]
    ms = jnp.mean(x * x, axis=-1, keepdims=True)
    h = (x * lax.rsqrt(ms + EPS) * gmix_ref[...]).astype(BF16)

    vf = _dot_nt(wvt_ref[...], h)
    ones_rows = jnp.where(lax.broadcasted_iota(jnp.int32, (V_ROWS - HEAD_DIM, tm), 0) == 0, 1.0, 0.0)
    for hd in range(n_heads):
        vt_ref[0, hd] = jnp.concatenate(
            [vf[hd * HEAD_DIM:(hd + 1) * HEAD_DIM], ones_rows], axis=0).astype(BF16)

    zf = vf[2 * attn_w:2 * attn_w + gate_rows] + bft_ref[...]
    c = jnp.minimum(zf, 0.0) - jnp.log1p(jnp.exp(-jnp.abs(zf)))
    pos = lax.broadcasted_iota(jnp.int32, (gate_rows, tm), 1)
    d = 1
    while d < tm:
        c = c + jnp.where(pos >= d, pltpu.roll(c, d, axis=1), 0.0)
        d *= 2
    fcum = c + fcarry_ref[:, :1]
    fcarry_ref[...] = jnp.broadcast_to(fcum[:, tm - 1:tm], fcarry_ref.shape)

    f_hi, f_mid, f_lo = _split3(fcum * LOG2_E)
    pieces = jnp.concatenate(
        [f_hi, f_mid, f_lo, jnp.zeros((LANES - 3 * gate_rows, tm), F32)], axis=0).astype(BF16)

    zg = _dot(h, wg_ref[...])
    gb = zg[:, :conv_ch]
    p = zg[:, conv_ch:2 * conv_ch] * zg[:, 2 * conv_ch:]
    yc = gb * _causal_conv3(p, pcarry_ref[...], wconv_ref[...])
    pcarry_ref[...] = p[tm - SUBLANES:]
    msc = jnp.mean(yc * yc, axis=-1, keepdims=True)
    mixc_ref[0] = (yc * lax.rsqrt(msc + EPS) * gconv_ref[...]).astype(BF16)

    routed = _dot(sel_ref[...], pieces)

    scale = LOG2_E / (HEAD_DIM ** 0.5)
    extra_row = lax.broadcasted_iota(jnp.int32, (LANES - HEAD_DIM, tm), 0)
    for hd in range(n_heads):
        q_extra = jnp.where((extra_row >= 6 * hd) & (extra_row < 6 * hd + 3), routed,
                            jnp.where((extra_row >= 6 * hd + 3) & (extra_row < 6 * hd + 6), 1.0, 0.0))
        qt_ref[0, hd] = jnp.concatenate(
            [vf[attn_w + hd * HEAD_DIM:attn_w + (hd + 1) * HEAD_DIM] * scale, q_extra], axis=0).astype(BF16)

    extras = jnp.concatenate([jnp.zeros((HEAD_DIM, tm), F32), routed], axis=0).T
    lane = lax.broadcasted_iota(jnp.int32, (tm, LANES), 1)
    for grp in range(n_heads // 4):
        zk = _dot(h, wk_ref[:, 2 * grp * LANES:2 * (grp + 1) * LANES])
        for hd in range(4 * grp, 4 * grp + 4):
            kb = zk[:, ((hd // 2) % 2) * LANES:((hd // 2) % 2 + 1) * LANES]
            if hd % 2:
                kb = pltpu.roll(kb, HEAD_DIM, axis=1)
            m0 = m0_ref[hd:hd + 1, :]
            m1 = m1_ref[hd:hd + 1, :]
            k_ref[0, hd] = jnp.where(lane < HEAD_DIM, kb, extras * m1 + m0).astype(BF16)


def _inproj(x, g_mix, w_in, b_f, w_conv, g_conv_out, *, n_heads, conv_ch):
    b, s, d = x.shape
    attn_w = n_heads * HEAD_DIM
    tm = ROW_TILE_IN
    gate_rows = -(-n_heads // SUBLANES) * SUBLANES
    extra = LANES - HEAD_DIM
    assert s % tm == 0 and n_heads % 4 == 0 and 6 * n_heads <= extra and 3 * gate_rows <= LANES
    o3 = 3 * conv_ch
    wg = w_in[:, :o3].astype(BF16)
    wk = w_in[:, o3 + attn_w:o3 + 2 * attn_w].astype(BF16)
    wvt = jnp.concatenate(
        [w_in[:, o3 + 2 * attn_w:o3 + 3 * attn_w].T, w_in[:, o3:o3 + attn_w].T,
         jnp.pad(w_in[:, o3 + 3 * attn_w:].T, ((0, 2 * SUBLANES - n_heads), (0, 0)))], axis=0).astype(BF16)
    bft = jnp.broadcast_to(jnp.pad(b_f, (0, gate_rows - n_heads))[:, None], (gate_rows, tm))
    wconv = jnp.pad(w_conv, ((0, SUBLANES - CONV_K), (0, 0)))
    hh = np.arange(n_heads)
    sel = np.zeros((extra, LANES), np.float32)
    m0 = np.zeros((n_heads, LANES), np.float32)
    m1 = np.zeros((n_heads, LANES), np.float32)
    for j in range(3):
        sel[6 * hh + j, j * gate_rows + hh] = 1.0
        sel[6 * hh + 3 + j, j * gate_rows + hh] = -1.0
        m0[hh, HEAD_DIM + 6 * hh + j] = 1.0
        m1[hh, HEAD_DIM + 6 * hh + 3 + j] = 1.0

    const = lambda shape: pl.BlockSpec(shape, lambda bi, i: (0,) * len(shape),
                                       pipeline_mode=pl.Buffered(1))
    head_rows = pl.BlockSpec((1, n_heads, tm, LANES), lambda bi, i: (bi, 0, i, 0))
    return pl.pallas_call(
        functools.partial(_inproj_kernel, n_heads=n_heads, conv_ch=conv_ch),
        grid=(b, s // tm),
        in_specs=[
            pl.BlockSpec((1, tm, d), lambda bi, i: (bi, i, 0)),
            const((1, d)), const(wg.shape), const(wk.shape), const(wvt.shape), const(bft.shape),
            const(sel.shape), const(m0.shape), const(m1.shape), const(wconv.shape), const((1, conv_ch)),
        ],
        out_specs=[
            pl.BlockSpec((1, tm, conv_ch), lambda bi, i: (bi, i, 0)),
            pl.BlockSpec((1, n_heads, LANES, tm), lambda bi, i: (bi, 0, 0, i)),
            head_rows,
            pl.BlockSpec((1, n_heads, V_ROWS, tm), lambda bi, i: (bi, 0, 0, i)),
        ],
        out_shape=[
            jax.ShapeDtypeStruct((b, s, conv_ch), BF16),
            jax.ShapeDtypeStruct((b, n_heads, LANES, s), BF16),
            jax.ShapeDtypeStruct((b, n_heads, s, LANES), BF16),
            jax.ShapeDtypeStruct((b, n_heads, V_ROWS, s), BF16),
        ],
        scratch_shapes=[pltpu.VMEM((SUBLANES, conv_ch), F32), pltpu.VMEM((gate_rows, LANES), F32)],
        compiler_params=pltpu.CompilerParams(
            dimension_semantics=("arbitrary", "arbitrary"),
            vmem_limit_bytes=_vmem_limit(
                ((tm, d), F32, 2), (wg.shape, BF16, 1), (wk.shape, BF16, 1), (wvt.shape, BF16, 1),
                ((tm, conv_ch), BF16, 2), ((n_heads, 2 * LANES + V_ROWS, tm), BF16, 2))),
        name="inproj",
    )(x, g_mix.reshape(1, d), wg, wk, wvt, bft, jnp.asarray(sel, BF16), jnp.asarray(m0), jnp.asarray(m1),
      wconv, g_conv_out.reshape(1, conv_ch))


def _attn_kernel(qt_ref, k_ref, vt_ref, wa_ref, wb_ref, wc_ref, o_ref, wa_out, wb_out, wc_out,
                 s_ref, acc_ref):
    nh = k_ref.shape[1]
    s = k_ref.shape[2]
    tq, tk = ATTN_Q_TILE, ATTN_K_TILE
    r = tq // tk
    for w_in, w_out in ((wa_ref, wa_out), (wb_ref, wb_out), (wc_ref, wc_out)):
        w_out[...] = w_in[...].astype(w_out.dtype)

    def causal(d):
        shape = (tk, tq - d * tk)
        return lax.broadcasted_iota(jnp.int32, shape, 0) <= lax.broadcasted_iota(jnp.int32, shape, 1)

    def load_q(qi):
        q0 = pl.multiple_of(qi * tq, tq)
        return [qt_ref[0, hh, :, pl.ds(q0, tq)] for hh in range(nh)]

    def issue_scores(j, slot, qs):
        k0 = pl.multiple_of(j * tk, tk)
        maxima = []
        for hh in range(nh):
            st = _dot(k_ref[0, hh, pl.ds(k0, tk), :], qs[hh])
            s_ref[hh, slot] = st
            maxima.append(jnp.max(st, axis=0, keepdims=True))
        return tuple(maxima)

    def q_tile(qi, first_max):
        q0 = pl.multiple_of(qi * tq, tq)
        qs = load_q(qi)

        def consume(j, slot, ms, tile_max):
            k0 = pl.multiple_of(j * tk, tk)
            out = []
            for hh in range(nh):
                m_new = jnp.maximum(ms[hh], tile_max[hh])
                p = jnp.exp2(s_ref[hh, slot] - m_new).astype(BF16)
                alpha = jnp.exp2(ms[hh] - m_new)
                acc_ref[hh] = acc_ref[hh] * alpha + _dot(vt_ref[0, hh, :, pl.ds(k0, tk)], p)
                out.append(m_new)
            return tuple(out)

        def kv_pair(jj, state):
            ms, max0 = state
            max1 = issue_scores(2 * jj + 1, 1, qs)
            ms = consume(2 * jj, 0, ms, max0)
            max0 = issue_scores(2 * jj + 2, 0, qs)
            return consume(2 * jj + 1, 1, ms, max1), max0

        acc_ref[...] = jnp.zeros_like(acc_ref)
        state = (tuple(jnp.full((1, tq), NEG_BIG, F32) for _ in range(nh)), first_max)
        def pairs(j0, n, st):
            for u in range(n):
                st = kv_pair(j0 + u, st)
            return st

        n_pairs = qi * (r // 2)
        done = 0
        group = 1
        while group < ATTN_PAIRS_PER_TRIP:
            take = group * ((n_pairs // group) % 2)
            state = lax.cond(take > 0, functools.partial(pairs, done, group), lambda st: st, state)
            done = done + take
            group *= 2
        ms, _ = lax.fori_loop(
            0, n_pairs // ATTN_PAIRS_PER_TRIP,
            lambda i, st: pairs(done + ATTN_PAIRS_PER_TRIP * i, ATTN_PAIRS_PER_TRIP, st), state)

        def diag_cols(d):
            return slice(d * tk, tq), pl.multiple_of((r * qi + d) * tk, tk)

        def issue_diag(d):
            cols, k0 = diag_cols(d)
            for hh in range(nh):
                s_ref[hh, d % 2, :, cols] = _dot(k_ref[0, hh, pl.ds(k0, tk), :], qs[hh][:, d * tk:])

        def consume_diag(d, ms):
            cols, k0 = diag_cols(d)
            out = []
            for hh in range(nh):
                st = jnp.where(causal(d), s_ref[hh, d % 2, :, cols], NEG_BIG)
                m_old = ms[hh][:, cols]
                m_new = jnp.maximum(m_old, jnp.max(st, axis=0, keepdims=True))
                p = jnp.exp2(st - m_new).astype(BF16)
                acc_ref[hh, :, cols] = (acc_ref[hh, :, cols] * jnp.exp2(m_old - m_new)
                                        + _dot(vt_ref[0, hh, :, pl.ds(k0, tk)], p))
                out.append(m_new if d == 0 else jnp.concatenate([ms[hh][:, :d * tk], m_new], axis=1))
            return tuple(out)

        issue_diag(1)
        for d in range(r):
            ms = consume_diag(d, ms)
            if d + 2 < r:
                issue_diag(d + 2)
            elif d == r - 2:
                next_max = issue_scores(0, 0, load_q(jnp.minimum(qi + 1, s // tq - 1)))
        for hh in range(nh):
            acc = acc_ref[hh]
            o = acc[:HEAD_DIM] / acc[HEAD_DIM:HEAD_DIM + 1]
            o_ref[0, hh * HEAD_DIM:(hh + 1) * HEAD_DIM, pl.ds(q0, tq)] = o.astype(o_ref.dtype)
        return next_max

    lax.fori_loop(0, s // tq, q_tile, issue_scores(0, 0, load_q(0)))


def _attention(qt_aug, k_aug, vt_aug, weights):
    b, n_heads, s, _ = k_aug.shape
    nh = ATTN_HEADS_PER_STEP
    tq, tk = ATTN_Q_TILE, ATTN_K_TILE
    n_hsteps = n_heads // nh
    n_steps = b * n_hsteps
    bf16_rows = 2 * SUBLANES
    assert s % tq == 0 and tq % (2 * tk) == 0 and n_heads % nh == 0
    assert all(w.shape[0] % (n_steps * bf16_rows) == 0 for w in weights)
    rows = pl.BlockSpec((1, nh, s, LANES), lambda bi, hi: (bi, hi, 0, 0))
    slabs = [pl.BlockSpec((w.shape[0] // n_steps, w.shape[1]), lambda bi, hi: (bi * n_hsteps + hi, 0))
             for w in weights]
    slab_bytes = [((w.shape[0] // n_steps, w.shape[1]), F32, 3) for w in weights]
    out = pl.pallas_call(
        _attn_kernel,
        grid=(b, n_hsteps),
        in_specs=[pl.BlockSpec((1, nh, LANES, s), lambda bi, hi: (bi, hi, 0, 0)), rows,
                  pl.BlockSpec((1, nh, V_ROWS, s), lambda bi, hi: (bi, hi, 0, 0))] + slabs,
        out_specs=[pl.BlockSpec((1, nh * HEAD_DIM, s), lambda bi, hi: (bi, hi, 0))] + slabs,
        out_shape=[jax.ShapeDtypeStruct((b, n_heads * HEAD_DIM, s), BF16)]
                  + [jax.ShapeDtypeStruct(w.shape, BF16) for w in weights],
        scratch_shapes=[pltpu.VMEM((nh, ATTN_SLOTS, tk, tq), F32), pltpu.VMEM((nh, V_ROWS, tq), F32)],
        compiler_params=pltpu.CompilerParams(
            dimension_semantics=("arbitrary", "arbitrary"),
            vmem_limit_bytes=_vmem_limit(
                ((nh, 2 * LANES + V_ROWS + HEAD_DIM, s), BF16, 2),
                ((nh, ATTN_SLOTS, tk, tq), F32, 1), ((nh, V_ROWS, tq), F32, 1), *slab_bytes)),
        name="fox_attention",
    )(qt_aug, k_aug, vt_aug, *weights)
    return out[0], out[1:]


def _post_kernel(x_ref, mixc_ref, yt_ref, gattn_ref, wo_ref, gffn_ref, wup_ref, wfc_ref, wdown_ref,
                 gfin_ref, out_ref, ucarry_ref, h2_ref, up_ref, act_ref, *, d_ff, final_norm):
    tm = x_ref.shape[1]

    @pl.when(pl.program_id(1) == 0)
    def _():
        ucarry_ref[...] = jnp.zeros_like(ucarry_ref)

    yt = yt_ref[0].astype(F32)
    msa = jnp.mean(yt * yt, axis=0, keepdims=True)
    yn = (yt * lax.rsqrt(msa + EPS) * gattn_ref[...]).T.astype(BF16)
    mix = jnp.concatenate([mixc_ref[0], yn], axis=1)
    x1 = x_ref[0] + _dot(mix, wo_ref[...])

    ms = jnp.mean(x1 * x1, axis=-1, keepdims=True)
    h2 = (x1 * lax.rsqrt(ms + EPS) * gffn_ref[...]).astype(BF16)

    out_ref[0] = x1
    h2_ref[...] = h2
    starts = list(range(0, d_ff, FF_CHUNK))
    n_chunks = len(starts)
    width = lambda c: min(FF_CHUNK, d_ff - starts[c])
    cols = lambda c: (slice(starts[c], starts[c] + width(c)),
                      slice(d_ff + starts[c], d_ff + starts[c] + width(c)))

    def issue_up(c):
        w = width(c)
        for half, cc in enumerate(cols(c)):
            up_ref[c % FF_UP_SLOTS, half, :SUBLANES, :w] = ucarry_ref[:, cc]
            up_ref[c % FF_UP_SLOTS, half, SUBLANES:, :w] = _dot(h2_ref[...], wup_ref[:, cc])

    def gate_and_down(c):
        ca, cg = cols(c)
        w = width(c)
        slot = c % FF_UP_SLOTS
        for r0 in range(0, tm, FF_ROW_BLOCK):
            ext = slice(r0, r0 + FF_ROW_BLOCK + SUBLANES)
            a = _causal_conv3_ext(up_ref[slot, 0, ext, :w], wfc_ref[:, ca])
            g = _causal_conv3_ext(up_ref[slot, 1, ext, :w], wfc_ref[:, cg])
            act_ref[c % 2, r0:r0 + FF_ROW_BLOCK, :w] = (g * jax.nn.sigmoid(g) * a).astype(BF16)
        ucarry_ref[:, ca] = up_ref[slot, 0, tm:, :w]
        ucarry_ref[:, cg] = up_ref[slot, 1, tm:, :w]
        acc = out_ref[0] + _dot(act_ref[c % 2, :, :w], wdown_ref[ca, :])
        if final_norm and c == n_chunks - 1:
            msf = jnp.mean(acc * acc, axis=-1, keepdims=True)
            acc = acc * lax.rsqrt(msf + EPS) * gfin_ref[...]
        out_ref[0] = acc

    for c in range(min(FF_UP_SLOTS - 1, n_chunks)):
        issue_up(c)
    for c in range(n_chunks):
        if c + FF_UP_SLOTS - 1 < n_chunks:
            issue_up(c + FF_UP_SLOTS - 1)
        gate_and_down(c)


def _post(x, mixc, yt, g_attn_out, w_o, g_ffn, w_up, w_ffn_conv, w_down, g_final, *, final_norm):
    b, s, d = x.shape
    conv_ch = mixc.shape[-1]
    attn_w = yt.shape[1]
    d_ff = w_down.shape[0]
    tm = ROW_TILE_POST
    assert s % tm == 0 and d_ff % LANES == 0 and FF_CHUNK % LANES == 0
    assert w_o.dtype == w_up.dtype == w_down.dtype == BF16
    wfc = jnp.pad(w_ffn_conv, ((0, SUBLANES - CONV_K), (0, 0)))

    const = lambda shape: pl.BlockSpec(shape, lambda bi, i: (0,) * len(shape),
                                       pipeline_mode=pl.Buffered(1))
    return pl.pallas_call(
        functools.partial(_post_kernel, d_ff=d_ff, final_norm=final_norm),
        grid=(b, s // tm),
        in_specs=[
            pl.BlockSpec((1, tm, d), lambda bi, i: (bi, i, 0)),
            pl.BlockSpec((1, tm, conv_ch), lambda bi, i: (bi, i, 0)),
            pl.BlockSpec((1, attn_w, tm), lambda bi, i: (bi, 0, i)),
            const((attn_w, 1)), const(w_o.shape), const((1, d)), const(w_up.shape),
            const(wfc.shape), const(w_down.shape), const((1, d)),
        ],
        out_specs=pl.BlockSpec((1, tm, d), lambda bi, i: (bi, i, 0)),
        out_shape=jax.ShapeDtypeStruct((b, s, d), F32),
        scratch_shapes=[pltpu.VMEM((SUBLANES, 2 * d_ff), F32), pltpu.VMEM((tm, d), BF16),
                        pltpu.VMEM((FF_UP_SLOTS, 2, tm + SUBLANES, FF_CHUNK), F32),
                        pltpu.VMEM((2, tm, FF_CHUNK), BF16)],
        compiler_params=pltpu.CompilerParams(
            dimension_semantics=("arbitrary", "arbitrary"),
            vmem_limit_bytes=_vmem_limit(
                ((tm, d), F32, 4), ((tm, conv_ch + attn_w), BF16, 2),
                (w_o.shape, BF16, 1), (w_up.shape, BF16, 1), (w_down.shape, BF16, 1),
                ((tm, d), BF16, 1), ((FF_UP_SLOTS, 2, tm + SUBLANES, FF_CHUNK), F32, 1),
                ((2, tm, FF_CHUNK), BF16, 1))),
        name="post_ffn",
    )(x, mixc, yt, g_attn_out.reshape(attn_w, 1), w_o, g_ffn.reshape(1, d), w_up, wfc, w_down,
      g_final.reshape(1, d))


def kernel(x, g_mix, w_in, b_f, w_conv, g_conv_out, g_attn_out, w_o, g_ffn, w_up, w_ffn_conv, w_down,
           g_final):
    depth = w_in.shape[0]
    conv_ch = w_conv.shape[-1]
    n_heads = b_f.shape[-1]
    for l in range(depth):
        mixc, qt_aug, k_aug, vt_aug = _inproj(x, g_mix[l], w_in[l], b_f[l], w_conv[l], g_conv_out[l],
                                             n_heads=n_heads, conv_ch=conv_ch)
        yt, (w_o_b, w_up_b, w_down_b) = _attention(qt_aug, k_aug, vt_aug, (w_o[l], w_up[l], w_down[l]))
        x = _post(x, mixc, yt, g_attn_out[l], w_o_b, g_ffn[l], w_up_b, w_ffn_conv[l], w_down_b,
                  g_final, final_norm=(l == depth - 1))
    return x
```
